```python
import jax, jax.numpy as jnp
from jax import lax
import numpy as np

D_MODEL = 1024
BATCH = 8
SEQ = 4096
DEPTH = 4

CHUNK = 64
N_A = DEPTH // 2
N_B = DEPTH - N_A
GMLP_BLOCK = 128
GMLP_WIDTH = D_MODEL
GMLP_GROUPS = 8
GMLP_GROUP_DIM = GMLP_WIDTH // GMLP_GROUPS
SB_HEADS = 16
SB_HEAD_DIM = D_MODEL // SB_HEADS
SB_QBLOCK = 128
D_FF = 4 * D_MODEL
ALPHA = float((2 * DEPTH) ** 0.25)
BETA = float((8 * DEPTH) ** -0.25)
LN_EPS = 1e-5

kernel_name = "yoco_gmlp_stickbreaking_deepnorm"


def layer_norm(x, g, b):
    xf = x.astype(jnp.float32)
    mu = jnp.mean(xf, axis=-1, keepdims=True)
    var = jnp.mean(jnp.square(xf - mu), axis=-1, keepdims=True)
    y = (xf - mu) * lax.rsqrt(var + LN_EPS) * g.astype(jnp.float32) + b.astype(jnp.float32)
    return y.astype(x.dtype)


def chunk_causal_mask(n):
    pos = jnp.arange(n)
    return (pos[None, :] // CHUNK) <= (pos[:, None] // CHUNK)


def gmlp_mixer(x, w_in, ln_g, ln_b, w_s, b_s, w_out):
    bsz, seq, _ = x.shape
    z = jax.nn.gelu(x @ w_in)
    u, v = jnp.split(z, 2, axis=-1)
    v = layer_norm(v, ln_g, ln_b)
    nblk = seq // GMLP_BLOCK
    v = v.reshape(bsz, nblk, GMLP_BLOCK, GMLP_GROUPS, GMLP_GROUP_DIM)
    ws = jnp.where(chunk_causal_mask(GMLP_BLOCK)[None], w_s, jnp.zeros((), w_s.dtype))
    s = jnp.einsum('gts,bnsgc->bntgc', ws, v)
    s = s + jnp.transpose(b_s)[None, None, :, :, None]
    s = s.reshape(bsz, seq, GMLP_WIDTH)
    return (u * s) @ w_out


def stick_breaking_mixer(x, w_q, w_o, k, v):
    bsz, seq, _ = x.shape
    q = (x @ w_q).reshape(bsz, seq, SB_HEADS, SB_HEAD_DIM).transpose(0, 2, 1, 3)
    scale = SB_HEAD_DIM ** -0.5
    outs = []
    for i in range(seq // SB_QBLOCK):
        q0, q1 = i * SB_QBLOCK, (i + 1) * SB_QBLOCK
        qb = q[:, :, q0:q1]
        kb = k[:, :, :q1]
        vb = v[:, :, :q1]
        zlog = jnp.einsum('bhtd,bhsd->bhts', qb, kb).astype(jnp.float32) * scale
        t_idx = q0 + jnp.arange(SB_QBLOCK)
        s_idx = jnp.arange(q1)
        causal = s_idx[None, :] < t_idx[:, None]
        log_rem = jnp.where(causal, jax.nn.log_sigmoid(-zlog), 0.0)
        excl = lax.cumsum(log_rem, axis=3, reverse=True) - log_rem
        log_a = jax.nn.log_sigmoid(zlog) + excl
        a = jnp.where(causal, jnp.exp(log_a), 0.0).astype(vb.dtype)
        outs.append(jnp.einsum('bhts,bhsd->bhtd', a, vb))
    o = jnp.concatenate(outs, axis=2)
    o = o.transpose(0, 2, 1, 3).reshape(bsz, seq, D_MODEL)
    return o @ w_o


def squared_relu_mlp(x, w1, w2):
    return jnp.square(jax.nn.relu(x @ w1)) @ w2


def setup_inputs(seed: int = 0) -> dict:
    key = jax.random.key(seed)
    ks = jax.random.split(key, 20)

    def nrm(k, shape, scale):
        return jax.random.normal(k, shape, jnp.float32) * scale

    return {
        "x": nrm(ks[0], (BATCH, SEQ, D_MODEL), 1.0),
        "a_w_in": nrm(ks[1], (N_A, D_MODEL, 2 * GMLP_WIDTH), D_MODEL ** -0.5),
        "a_ln_g": 1.0 + nrm(ks[2], (N_A, GMLP_WIDTH), 0.02),
        "a_ln_b": nrm(ks[3], (N_A, GMLP_WIDTH), 0.02),
        "a_w_s": nrm(ks[4], (N_A, GMLP_GROUPS, GMLP_BLOCK, GMLP_BLOCK), GMLP_BLOCK ** -0.5),
        "a_b_s": 1.0 + nrm(ks[5], (N_A, GMLP_GROUPS, GMLP_BLOCK), 0.02),
        "a_w_out": nrm(ks[6], (N_A, GMLP_WIDTH, D_MODEL), BETA * GMLP_WIDTH ** -0.5),
        "sb_w_k": nrm(ks[7], (D_MODEL, D_MODEL), D_MODEL ** -0.5),
        "sb_w_v": nrm(ks[8], (D_MODEL, D_MODEL), BETA * D_MODEL ** -0.5),
        "b_w_q": nrm(ks[9], (N_B, D_MODEL, D_MODEL), D_MODEL ** -0.5),
        "b_w_o": nrm(ks[10], (N_B, D_MODEL, D_MODEL), BETA * D_MODEL ** -0.5),
        "mix_ln_g": 1.0 + nrm(ks[11], (DEPTH, D_MODEL), 0.02),
        "mix_ln_b": nrm(ks[12], (DEPTH, D_MODEL), 0.02),
        "ffn_ln_g": 1.0 + nrm(ks[13], (DEPTH, D_MODEL), 0.02),
        "ffn_ln_b": nrm(ks[14], (DEPTH, D_MODEL), 0.02),
        "ffn_w1": nrm(ks[15], (DEPTH, D_MODEL, D_FF), BETA * D_MODEL ** -0.5),
        "ffn_w2": nrm(ks[16], (DEPTH, D_FF, D_MODEL), BETA * D_FF ** -0.5),
    }


def reference(x, a_w_in, a_ln_g, a_ln_b, a_w_s, a_b_s, a_w_out, sb_w_k, sb_w_v,
              b_w_q, b_w_o, mix_ln_g, mix_ln_b, ffn_ln_g, ffn_ln_b, ffn_w1, ffn_w2):
    bsz, seq, _ = x.shape
    k_shared = None
    v_shared = None
    for l in range(DEPTH):
        if l < N_A:
            mix = gmlp_mixer(x, a_w_in[l], a_ln_g[l], a_ln_b[l], a_w_s[l], a_b_s[l], a_w_out[l])
        else:
            if l == N_A:
                k_shared = (x @ sb_w_k).reshape(bsz, seq, SB_HEADS, SB_HEAD_DIM).transpose(0, 2, 1, 3)
                v_shared = (x @ sb_w_v).reshape(bsz, seq, SB_HEADS, SB_HEAD_DIM).transpose(0, 2, 1, 3)
            j = l - N_A
            mix = stick_breaking_mixer(x, b_w_q[j], b_w_o[j], k_shared, v_shared)
        x = layer_norm(ALPHA * x + mix, mix_ln_g[l], mix_ln_b[l])
        x = layer_norm(ALPHA * x + squared_relu_mlp(x, ffn_w1[l], ffn_w2[l]), ffn_ln_g[l], ffn_ln_b[l])
    return x
```

```python
import functools

import jax
import jax.numpy as jnp
from jax import lax
from jax.experimental import pallas as pl
from jax.experimental.pallas import tpu as pltpu

F32 = jnp.float32
BF16 = jnp.bfloat16

CHUNK = 64
GMLP_BLOCK = 128
GMLP_GROUPS = 8
SB_HEADS = 16
LN_EPS = 1e-5
DEPTH = 4
N_A = DEPTH // 2
ALPHA = float((2 * DEPTH) ** 0.25)

F32_EXP_ZERO = -105.0

VMEM_LIMIT = 56 * 1024 * 1024


def _layer_norm(y, g, b):
    mu = jnp.mean(y, axis=-1, keepdims=True)
    d = y - mu
    var = jnp.mean(d * d, axis=-1, keepdims=True)
    return d * lax.rsqrt(var + LN_EPS) * g + b


def _gelu_tanh(x):
    c = 0.7978845608028654
    return 0.5 * x * (1.0 + jnp.tanh(c * (x + 0.044715 * (x * x * x))))


def _const_spec(shape):
    zeros = (0,) * len(shape)
    return pl.BlockSpec(shape, lambda *_: zeros, pipeline_mode=pl.Buffered(1))


def _gmlp_kernel(x_ref, win_ref, lng_ref, lnb_ref, ws_ref, bs_ref, wout_ref,
                 mg_ref, mb_ref, o_ref, gated_ref):
    tm, d = x_ref.shape
    w = wout_ref.shape[0]
    gd = w // GMLP_GROUPS
    x = x_ref[...]
    z = jnp.dot(x.astype(BF16), win_ref[...], preferred_element_type=F32)
    z = _gelu_tanh(z)
    u = z[:, :w]
    v = _layer_norm(z[:, w:], lng_ref[...], lnb_ref[...]).astype(BF16)
    t_idx = lax.broadcasted_iota(jnp.int32, (GMLP_BLOCK, GMLP_BLOCK), 0)
    s_idx = lax.broadcasted_iota(jnp.int32, (GMLP_BLOCK, GMLP_BLOCK), 1)
    allowed = (s_idx // CHUNK) <= (t_idx // CHUNK)
    for g in range(GMLP_GROUPS):
        ws = jnp.where(allowed, ws_ref[g], 0.0).astype(BF16)
        bias = bs_ref[:, g:g + 1]
        for r in range(tm // GMLP_BLOCK):
            rows = slice(r * GMLP_BLOCK, (r + 1) * GMLP_BLOCK)
            cols = slice(g * gd, (g + 1) * gd)
            s = jnp.dot(ws, v[rows, cols], preferred_element_type=F32) + bias
            gated_ref[rows, cols] = (u[rows, cols] * s).astype(BF16)
    mix = jnp.dot(gated_ref[...], wout_ref[...], preferred_element_type=F32)
    o_ref[...] = _layer_norm(ALPHA * x + mix, mg_ref[...], mb_ref[...])


def _gmlp_layer(x2, w_in, ln_g, ln_b, w_s, b_s_t, w_out, mg, mb, *, tm):
    n, d = x2.shape
    w = w_out.shape[0]
    row = pl.BlockSpec((tm, d), lambda i: (i, 0))
    return pl.pallas_call(
        _gmlp_kernel,
        grid=(n // tm,),
        in_specs=[row, _const_spec((d, 2 * w)), _const_spec((1, w)), _const_spec((1, w)),
                  _const_spec(w_s.shape), _const_spec(b_s_t.shape), _const_spec((w, d)),
                  _const_spec((1, d)), _const_spec((1, d))],
        out_specs=row,
        out_shape=jax.ShapeDtypeStruct((n, d), F32),
        scratch_shapes=[pltpu.VMEM((tm, w), BF16)],
        compiler_params=pltpu.CompilerParams(
            dimension_semantics=("arbitrary",), vmem_limit_bytes=VMEM_LIMIT),
        name="gmlp_mixer",
    )(x2, w_in, ln_g, ln_b, w_s, b_s_t, w_out, mg, mb)


def _mlp_kernel(x_ref, w1_ref, w2_ref, g_ref, b_ref, o_ref, *, ff_chunk):
    x = x_ref[...]
    x16 = x.astype(BF16)
    d_ff = w1_ref.shape[1]
    acc = None
    for c in range(d_ff // ff_chunk):
        cols = slice(c * ff_chunk, (c + 1) * ff_chunk)
        h = jnp.dot(x16, w1_ref[:, cols], preferred_element_type=F32)
        h = jnp.maximum(h, 0.0)
        h = (h * h).astype(BF16)
        part = jnp.dot(h, w2_ref[cols, :], preferred_element_type=F32)
        acc = part if acc is None else acc + part
    o_ref[...] = _layer_norm(ALPHA * x + acc, g_ref[...], b_ref[...])


def _mlp_layer(x2, w1, w2, g, b, *, tm, ff_chunk):
    n, d = x2.shape
    d_ff = w1.shape[1]
    row = pl.BlockSpec((tm, d), lambda i: (i, 0))
    return pl.pallas_call(
        functools.partial(_mlp_kernel, ff_chunk=ff_chunk),
        grid=(n // tm,),
        in_specs=[row, _const_spec((d, d_ff)), _const_spec((d_ff, d)),
                  _const_spec((1, d)), _const_spec((1, d))],
        out_specs=row,
        out_shape=jax.ShapeDtypeStruct((n, d), F32),
        compiler_params=pltpu.CompilerParams(
            dimension_semantics=("arbitrary",), vmem_limit_bytes=VMEM_LIMIT),
        name="relu2_mlp",
    )(x2, w1, w2, g, b)


def _kv_kernel(x_ref, wkt_ref, wv_ref, kt_ref, v_ref):
    x16 = x_ref[0].astype(BF16)
    kt = lax.dot_general(wkt_ref[...], x16, (((1,), (1,)), ((), ())),
                         preferred_element_type=F32)
    kt_ref[0] = kt.astype(BF16)
    v_ref[0] = jnp.dot(x16, wv_ref[...], preferred_element_type=F32).astype(BF16)


def _kv_proj(x3, wk_t, wv, *, tm):
    b, s, d = x3.shape
    return pl.pallas_call(
        _kv_kernel,
        grid=(b, s // tm),
        in_specs=[pl.BlockSpec((1, tm, d), lambda i, j: (i, j, 0)),
                  _const_spec((d, d)), _const_spec((d, d))],
        out_specs=[pl.BlockSpec((1, d, tm), lambda i, j: (i, 0, j)),
                   pl.BlockSpec((1, tm, d), lambda i, j: (i, j, 0))],
        out_shape=[jax.ShapeDtypeStruct((b, d, s), BF16),
                   jax.ShapeDtypeStruct((b, s, d), BF16)],
        compiler_params=pltpu.CompilerParams(
            dimension_semantics=("arbitrary", "arbitrary"), vmem_limit_bytes=VMEM_LIMIT),
        name="kv_proj",
    )(x3, wk_t, wv)


def _sb_kernel(x_ref, wq_ref, kt_ref, v_ref, wo_ref, g_ref, b_ref, o_ref, att_ref, *, tq):
    d = x_ref.shape[2]
    dh = d // SB_HEADS
    scale = dh ** -0.5
    qi = pl.program_id(1)
    x = x_ref[0]
    q = jnp.dot(x.astype(BF16), wq_ref[...], preferred_element_type=F32)
    q = (q * scale).astype(BF16)

    row = lax.broadcasted_iota(jnp.int32, (tq, tq), 0)
    col = lax.broadcasted_iota(jnp.int32, (tq, tq), 1)
    tri = (row > col).astype(BF16)

    for h in range(SB_HEADS):
        hs = slice(h * dh, (h + 1) * dh)
        q_h = q[:, hs]

        def body(carry, hs=hs, q_h=q_h):
            kb, _, rem, acc = carry
            k0 = pl.multiple_of(kb * tq, tq)
            kt_h = kt_ref[0, hs, pl.ds(k0, tq)]
            z = jnp.dot(q_h, kt_h, preferred_element_type=F32)
            causal = (col + (kb - qi) * tq) < row
            lr = -(jnp.maximum(z, 0.0) + jnp.log1p(jnp.exp(-jnp.abs(z))))
            ls = lr + z
            lr = jnp.where(causal, lr, 0.0)
            hi = lr.astype(BF16)
            lo = (lr - hi.astype(F32)).astype(BF16)
            excl = (jnp.dot(hi, tri, preferred_element_type=F32)
                    + jnp.dot(lo, tri, preferred_element_type=F32))
            log_a = ls + excl + rem
            a = jnp.where(causal, jnp.exp(log_a), 0.0).astype(BF16)
            v_h = v_ref[0, pl.ds(k0, tq), hs]
            acc = acc + jnp.dot(a, v_h, preferred_element_type=F32)
            rem = rem + jnp.sum(lr, axis=1, keepdims=True)
            go = (jnp.max(rem) > F32_EXP_ZERO).astype(jnp.int32)
            return kb - 1, go, rem, acc

        def cond(carry):
            kb, go, _, _ = carry
            return jnp.logical_and(kb >= 0, go > 0)

        init = (qi, jnp.int32(1), jnp.zeros((tq, 1), F32), jnp.zeros((tq, dh), F32))
        _, _, _, acc = lax.while_loop(cond, body, init)
        att_ref[:, hs] = acc.astype(BF16)

    mix = jnp.dot(att_ref[...], wo_ref[...], preferred_element_type=F32)
    o_ref[0] = _layer_norm(ALPHA * x + mix, g_ref[...], b_ref[...])


def _sb_layer(x3, wq, kt, v, wo, g, b, *, tq):
    bsz, s, d = x3.shape
    row = pl.BlockSpec((1, tq, d), lambda i, j: (i, j, 0))
    return pl.pallas_call(
        functools.partial(_sb_kernel, tq=tq),
        grid=(bsz, s // tq),
        in_specs=[row, _const_spec((d, d)),
                  pl.BlockSpec((1, d, s), lambda i, j: (i, 0, 0), pipeline_mode=pl.Buffered(1)),
                  pl.BlockSpec((1, s, d), lambda i, j: (i, 0, 0), pipeline_mode=pl.Buffered(1)),
                  _const_spec((d, d)), _const_spec((1, d)), _const_spec((1, d))],
        out_specs=row,
        out_shape=jax.ShapeDtypeStruct((bsz, s, d), F32),
        scratch_shapes=[pltpu.VMEM((tq, d), BF16)],
        compiler_params=pltpu.CompilerParams(
            dimension_semantics=("arbitrary", "arbitrary"), vmem_limit_bytes=VMEM_LIMIT),
        name="sb_mixer",
    )(x3, wq, kt, v, wo, g, b)


def kernel(x, a_w_in, a_ln_g, a_ln_b, a_w_s, a_b_s, a_w_out, sb_w_k, sb_w_v, b_w_q, b_w_o,
           mix_ln_g, mix_ln_b, ffn_ln_g, ffn_ln_b, ffn_w1, ffn_w2):
    bsz, seq, d = x.shape
    n = bsz * seq
    vec = lambda p: p.reshape(1, -1).astype(F32)
    x2 = x.reshape(n, d)
    kt = v = None
    for l in range(DEPTH):
        if l < N_A:
            x2 = _gmlp_layer(
                x2, a_w_in[l].astype(BF16), vec(a_ln_g[l]), vec(a_ln_b[l]), a_w_s[l],
                jnp.transpose(a_b_s[l]), a_w_out[l].astype(BF16),
                vec(mix_ln_g[l]), vec(mix_ln_b[l]), tm=256)
        else:
            x3 = x2.reshape(bsz, seq, d)
            if l == N_A:
                kt, v = _kv_proj(x3, jnp.transpose(sb_w_k).astype(BF16),
                                 sb_w_v.astype(BF16), tm=512)
            j = l - N_A
            x2 = _sb_layer(x3, b_w_q[j].astype(BF16), kt, v, b_w_o[j].astype(BF16),
                           vec(mix_ln_g[l]), vec(mix_ln_b[l]), tq=128).reshape(n, d)
        x2 = _mlp_layer(x2, ffn_w1[l].astype(BF16), ffn_w2[l].astype(BF16),
                        vec(ffn_ln_g[l]), vec(ffn_ln_b[l]), tm=512, ff_chunk=1024)
    return x2.reshape(bsz, seq, d)
```

```python
import functools

import jax
import jax.numpy as jnp
from jax import lax
from jax.experimental import pallas as pl
from jax.experimental.pallas import tpu as pltpu

F32 = jnp.float32
BF16 = jnp.bfloat16

LANES = 128
CHUNK = 64
GMLP_BLOCK = 128
GMLP_GROUPS = 8
SB_HEADS = 16
LN_EPS = 1e-5
DEPTH = 4
N_A = DEPTH // 2
ALPHA = float((2 * DEPTH) ** 0.25)

LOG2E = 1.4426950408889634
F32_EXP2_ZERO = 152.0

VMEM_LIMIT = 56 * 1024 * 1024


def _layer_norm(y, g, b):
    mu = jnp.mean(y, axis=-1, keepdims=True)
    d = y - mu
    var = jnp.mean(d * d, axis=-1, keepdims=True)
    return d * lax.rsqrt(var + LN_EPS) * g + b


def _gelu_tanh(x):
    c = 0.7978845608028654
    return 0.5 * x * (1.0 + jnp.tanh(c * (x + 0.044715 * (x * x * x))))


def _const_spec(shape):
    zeros = (0,) * len(shape)
    return pl.BlockSpec(shape, lambda *_: zeros, pipeline_mode=pl.Buffered(1))


def _gmlp_kernel(x_ref, win_ref, lng_ref, lnb_ref, ws_ref, bs_ref, wout_ref,
                 mg_ref, mb_ref, o_ref, gated_ref):
    tm, d = x_ref.shape
    w = wout_ref.shape[0]
    gd = w // GMLP_GROUPS
    x = x_ref[...]
    z = jnp.dot(x.astype(BF16), win_ref[...], preferred_element_type=F32)
    z = _gelu_tanh(z)
    u = z[:, :w]
    v = _layer_norm(z[:, w:], lng_ref[...], lnb_ref[...]).astype(BF16)
    t_idx = lax.broadcasted_iota(jnp.int32, (GMLP_BLOCK, GMLP_BLOCK), 0)
    s_idx = lax.broadcasted_iota(jnp.int32, (GMLP_BLOCK, GMLP_BLOCK), 1)
    allowed = (s_idx // CHUNK) <= (t_idx // CHUNK)
    for g in range(GMLP_GROUPS):
        ws = jnp.where(allowed, ws_ref[g], 0.0).astype(BF16)
        bias = bs_ref[:, g:g + 1]
        for r in range(tm // GMLP_BLOCK):
            rows = slice(r * GMLP_BLOCK, (r + 1) * GMLP_BLOCK)
            cols = slice(g * gd, (g + 1) * gd)
            s = jnp.dot(ws, v[rows, cols], preferred_element_type=F32) + bias
            gated_ref[rows, cols] = (u[rows, cols] * s).astype(BF16)
    mix = jnp.dot(gated_ref[...], wout_ref[...], preferred_element_type=F32)
    o_ref[...] = _layer_norm(ALPHA * x + mix, mg_ref[...], mb_ref[...])


def _gmlp_layer(x2, w_in, ln_g, ln_b, w_s, b_s_t, w_out, mg, mb, *, tm):
    n, d = x2.shape
    w = w_out.shape[0]
    row = pl.BlockSpec((tm, d), lambda i: (i, 0))
    return pl.pallas_call(
        _gmlp_kernel,
        grid=(n // tm,),
        in_specs=[row, _const_spec((d, 2 * w)), _const_spec((1, w)), _const_spec((1, w)),
                  _const_spec(w_s.shape), _const_spec(b_s_t.shape), _const_spec((w, d)),
                  _const_spec((1, d)), _const_spec((1, d))],
        out_specs=row,
        out_shape=jax.ShapeDtypeStruct((n, d), F32),
        scratch_shapes=[pltpu.VMEM((tm, w), BF16)],
        compiler_params=pltpu.CompilerParams(
            dimension_semantics=("arbitrary",), vmem_limit_bytes=VMEM_LIMIT),
        name="gmlp_mixer",
    )(x2, w_in, ln_g, ln_b, w_s, b_s_t, w_out, mg, mb)


def _mlp_kernel(x_ref, w1_ref, w2_ref, g_ref, b_ref, o_ref, *, ff_chunk):
    x = x_ref[...]
    x16 = x.astype(BF16)
    d_ff = w1_ref.shape[1]
    acc = None
    for c in range(d_ff // ff_chunk):
        cols = slice(c * ff_chunk, (c + 1) * ff_chunk)
        h = jnp.dot(x16, w1_ref[:, cols], preferred_element_type=F32)
        h = jnp.maximum(h, 0.0)
        h = (h * h).astype(BF16)
        part = jnp.dot(h, w2_ref[cols, :], preferred_element_type=F32)
        acc = part if acc is None else acc + part
    o_ref[...] = _layer_norm(ALPHA * x + acc, g_ref[...], b_ref[...])


def _mlp_layer(x2, w1, w2, g, b, *, tm, ff_chunk):
    n, d = x2.shape
    d_ff = w1.shape[1]
    row = pl.BlockSpec((tm, d), lambda i: (i, 0))
    return pl.pallas_call(
        functools.partial(_mlp_kernel, ff_chunk=ff_chunk),
        grid=(n // tm,),
        in_specs=[row, _const_spec((d, d_ff)), _const_spec((d_ff, d)),
                  _const_spec((1, d)), _const_spec((1, d))],
        out_specs=row,
        out_shape=jax.ShapeDtypeStruct((n, d), F32),
        compiler_params=pltpu.CompilerParams(
            dimension_semantics=("arbitrary",), vmem_limit_bytes=VMEM_LIMIT),
        name="relu2_mlp",
    )(x2, w1, w2, g, b)


def _kv_kernel(x_ref, wkt_ref, wv_ref, kt_ref, v_ref):
    x16 = x_ref[0].astype(BF16)
    kt = lax.dot_general(wkt_ref[...], x16, (((1,), (1,)), ((), ())),
                         preferred_element_type=F32)
    kt_ref[0] = kt.astype(BF16)
    v_ref[0] = jnp.dot(x16, wv_ref[...], preferred_element_type=F32).astype(BF16)


def _kv_proj(x3, wk_t, wv, *, tm):
    b, s, d = x3.shape
    return pl.pallas_call(
        _kv_kernel,
        grid=(b, s // tm),
        in_specs=[pl.BlockSpec((1, tm, d), lambda i, j: (i, j, 0)),
                  _const_spec((d, d)), _const_spec((d, d))],
        out_specs=[pl.BlockSpec((1, d, tm), lambda i, j: (i, 0, j)),
                   pl.BlockSpec((1, tm, d), lambda i, j: (i, j, 0))],
        out_shape=[jax.ShapeDtypeStruct((b, d, s), BF16),
                   jax.ShapeDtypeStruct((b, s, d), BF16)],
        compiler_params=pltpu.CompilerParams(
            dimension_semantics=("arbitrary", "arbitrary"), vmem_limit_bytes=VMEM_LIMIT),
        name="kv_proj",
    )(x3, wk_t, wv)


def _sb_kernel(x_ref, wq_ref, kt_ref, v_ref, wo_ref, g_ref, b_ref, o_ref,
               q_ref, acc_ref, rem_ref, *, tq):
    d = x_ref.shape[2]
    dh = d // SB_HEADS
    assert 2 * dh == LANES and tq == LANES
    scale = dh ** -0.5
    qi = pl.program_id(1)
    x = x_ref[0]
    q = jnp.dot(x.astype(BF16), wq_ref[...], preferred_element_type=F32)
    q_ref[...] = (q * scale).astype(BF16)

    row = lax.broadcasted_iota(jnp.int32, (tq, tq), 0)
    col = lax.broadcasted_iota(jnp.int32, (tq, tq), 1)
    causal = col < row
    r2 = lax.broadcasted_iota(jnp.int32, (2 * tq, 2 * tq), 0)
    c2 = lax.broadcasted_iota(jnp.int32, (2 * tq, 2 * tq), 1)
    tri2 = jnp.logical_or(c2 >= tq, (r2 & (tq - 1)) > c2).astype(BF16)
    first_head = lax.broadcasted_iota(jnp.int32, (tq, LANES), 1) < dh
    kt_zeros = jnp.zeros((dh, tq), BF16)

    def key_block(k0, diag):
        heads = range(SB_HEADS)
        zs = []
        for h in heads:
            p, j = divmod(h, 2)
            q_p = q_ref[:, p * LANES:(p + 1) * LANES]
            kt_h = kt_ref[0, h * dh:(h + 1) * dh, pl.ds(k0, tq)]
            kt_h = jnp.concatenate([kt_h, kt_zeros] if j == 0 else [kt_zeros, kt_h], axis=0)
            zs.append(jnp.dot(q_p, kt_h, preferred_element_type=F32) * LOG2E)
        log_betas, splits = [], []
        for h in heads:
            z = zs[h]
            sp = jnp.maximum(z, 0.0) + jnp.log2(1.0 + jnp.exp2(-jnp.abs(z)))
            log_betas.append(z - sp)
            if diag:
                sp = jnp.where(causal, sp, 0.0)
            hi = sp.astype(BF16)
            lo = (sp - hi.astype(F32)).astype(BF16)
            splits.append(jnp.concatenate([hi, lo], axis=1))
        sums = [jnp.dot(splits[h], tri2, preferred_element_type=F32) for h in heads]
        weights = []
        for h in heads:
            suffix, total = sums[h][:, :tq], sums[h][:, tq:]
            if diag:
                a = jnp.where(causal, jnp.exp2(log_betas[h] - suffix), 0.0)
                rem_ref[h] = total
            else:
                rem = rem_ref[h]
                a = jnp.exp2(log_betas[h] - (suffix + rem))
                rem_ref[h] = rem + total
            weights.append(a.astype(BF16))
        for p in range(SB_HEADS // 2):
            lanes = slice(p * LANES, (p + 1) * LANES)
            v_p = v_ref[0, pl.ds(k0, tq), lanes]
            v_2 = jnp.concatenate([jnp.where(first_head, v_p, 0), jnp.where(first_head, 0, v_p)],
                                  axis=0).astype(BF16)
            av = jnp.dot(jnp.concatenate(weights[2 * p:2 * p + 2], axis=1), v_2,
                         preferred_element_type=F32)
            if diag:
                acc_ref[:, lanes] = av
            else:
                acc_ref[:, lanes] += av

    def unfinished():
        m = rem_ref[0]
        for h in range(1, SB_HEADS):
            m = jnp.minimum(m, rem_ref[h])
        return (jnp.min(m) < F32_EXP2_ZERO).astype(jnp.int32)

    key_block(pl.multiple_of(qi * tq, tq), True)

    def body(carry):
        kb, _ = carry
        key_block(pl.multiple_of(kb * tq, tq), False)
        return kb - 1, unfinished()

    def cond(carry):
        kb, go = carry
        return jnp.logical_and(kb >= 0, go > 0)

    lax.while_loop(cond, body, (qi - 1, unfinished()))

    mix = jnp.dot(acc_ref[...].astype(BF16), wo_ref[...], preferred_element_type=F32)
    o_ref[0] = _layer_norm(ALPHA * x + mix, g_ref[...], b_ref[...])


def _sb_layer(x3, wq, kt, v, wo, g, b, *, tq):
    bsz, s, d = x3.shape
    row = pl.BlockSpec((1, tq, d), lambda i, j: (i, j, 0))
    return pl.pallas_call(
        functools.partial(_sb_kernel, tq=tq),
        grid=(bsz, s // tq),
        in_specs=[row, _const_spec((d, d)),
                  pl.BlockSpec((1, d, s), lambda i, j: (i, 0, 0), pipeline_mode=pl.Buffered(1)),
                  pl.BlockSpec((1, s, d), lambda i, j: (i, 0, 0), pipeline_mode=pl.Buffered(1)),
                  _const_spec((d, d)), _const_spec((1, d)), _const_spec((1, d))],
        out_specs=row,
        out_shape=jax.ShapeDtypeStruct((bsz, s, d), F32),
        scratch_shapes=[pltpu.VMEM((tq, d), BF16), pltpu.VMEM((tq, d), F32),
                        pltpu.VMEM((SB_HEADS, tq, LANES), F32)],
        compiler_params=pltpu.CompilerParams(
            dimension_semantics=("arbitrary", "arbitrary"), vmem_limit_bytes=VMEM_LIMIT),
        name="sb_mixer",
    )(x3, wq, kt, v, wo, g, b)


def kernel(x, a_w_in, a_ln_g, a_ln_b, a_w_s, a_b_s, a_w_out, sb_w_k, sb_w_v, b_w_q, b_w_o,
           mix_ln_g, mix_ln_b, ffn_ln_g, ffn_ln_b, ffn_w1, ffn_w2):
    bsz, seq, d = x.shape
    n = bsz * seq
    vec = lambda p: p.reshape(1, -1).astype(F32)
    x2 = x.reshape(n, d)
    kt = v = None
    for l in range(DEPTH):
        if l < N_A:
            x2 = _gmlp_layer(
                x2, a_w_in[l].astype(BF16), vec(a_ln_g[l]), vec(a_ln_b[l]), a_w_s[l],
                jnp.transpose(a_b_s[l]), a_w_out[l].astype(BF16),
                vec(mix_ln_g[l]), vec(mix_ln_b[l]), tm=256)
        else:
            x3 = x2.reshape(bsz, seq, d)
            if l == N_A:
                kt, v = _kv_proj(x3, jnp.transpose(sb_w_k).astype(BF16),
                                 sb_w_v.astype(BF16), tm=512)
            j = l - N_A
            x2 = _sb_layer(x3, b_w_q[j].astype(BF16), kt, v, b_w_o[j].astype(BF16),
                           vec(mix_ln_g[l]), vec(mix_ln_b[l]), tq=128).reshape(n, d)
        x2 = _mlp_layer(x2, ffn_w1[l].astype(BF16), ffn_w2[l].astype(BF16),
                        vec(ffn_ln_g[l]), vec(ffn_ln_b[l]), tm=512, ff_chunk=1024)
    return x2.reshape(bsz, seq, d)
```

```python
import functools

import jax
import jax.numpy as jnp
from jax import lax
from jax.experimental import pallas as pl
from jax.experimental.pallas import tpu as pltpu

F32 = jnp.float32
BF16 = jnp.bfloat16

LANES = 128
CHUNK = 64
GMLP_BLOCK = 128
GMLP_GROUPS = 8
SB_HEADS = 16
LN_EPS = 1e-5
DEPTH = 4
N_A = DEPTH // 2
ALPHA = float((2 * DEPTH) ** 0.25)

LOG2E = 1.4426950408889634
F32_EXP2_ZERO = 127.0
SB_KEY_BLOCK = 128

VMEM_LIMIT = 56 * 1024 * 1024


def _layer_norm(y, g, b):
    mu = jnp.mean(y, axis=-1, keepdims=True)
    d = y - mu
    var = jnp.mean(d * d, axis=-1, keepdims=True)
    return d * lax.rsqrt(var + LN_EPS) * g + b


def _gelu_tanh(x):
    c = 0.7978845608028654
    return 0.5 * x * (1.0 + jnp.tanh(c * (x + 0.044715 * (x * x * x))))


def _const_spec(shape):
    zeros = (0,) * len(shape)
    return pl.BlockSpec(shape, lambda *_: zeros, pipeline_mode=pl.Buffered(1))


def _gmlp_kernel(x_ref, win_ref, lng_ref, lnb_ref, ws_ref, bs_ref, wout_ref,
                 mg_ref, mb_ref, o_ref, gated_ref):
    tm, d = x_ref.shape
    w = wout_ref.shape[0]
    gd = w // GMLP_GROUPS
    x = x_ref[...]
    z = jnp.dot(x.astype(BF16), win_ref[...], preferred_element_type=F32)
    z = _gelu_tanh(z)
    u = z[:, :w]
    v = _layer_norm(z[:, w:], lng_ref[...], lnb_ref[...]).astype(BF16)
    t_idx = lax.broadcasted_iota(jnp.int32, (GMLP_BLOCK, GMLP_BLOCK), 0)
    s_idx = lax.broadcasted_iota(jnp.int32, (GMLP_BLOCK, GMLP_BLOCK), 1)
    allowed = (s_idx // CHUNK) <= (t_idx // CHUNK)
    for g in range(GMLP_GROUPS):
        ws = jnp.where(allowed, ws_ref[g], 0.0).astype(BF16)
        bias = bs_ref[:, g:g + 1]
        for r in range(tm // GMLP_BLOCK):
            rows = slice(r * GMLP_BLOCK, (r + 1) * GMLP_BLOCK)
            cols = slice(g * gd, (g + 1) * gd)
            s = jnp.dot(ws, v[rows, cols], preferred_element_type=F32) + bias
            gated_ref[rows, cols] = (u[rows, cols] * s).astype(BF16)
    mix = jnp.dot(gated_ref[...], wout_ref[...], preferred_element_type=F32)
    o_ref[...] = _layer_norm(ALPHA * x + mix, mg_ref[...], mb_ref[...])


def _gmlp_layer(x2, w_in, ln_g, ln_b, w_s, b_s_t, w_out, mg, mb, *, tm):
    n, d = x2.shape
    w = w_out.shape[0]
    row = pl.BlockSpec((tm, d), lambda i: (i, 0))
    return pl.pallas_call(
        _gmlp_kernel,
        grid=(n // tm,),
        in_specs=[row, _const_spec((d, 2 * w)), _const_spec((1, w)), _const_spec((1, w)),
                  _const_spec(w_s.shape), _const_spec(b_s_t.shape), _const_spec((w, d)),
                  _const_spec((1, d)), _const_spec((1, d))],
        out_specs=row,
        out_shape=jax.ShapeDtypeStruct((n, d), F32),
        scratch_shapes=[pltpu.VMEM((tm, w), BF16)],
        compiler_params=pltpu.CompilerParams(
            dimension_semantics=("arbitrary",), vmem_limit_bytes=VMEM_LIMIT),
        name="gmlp_mixer",
    )(x2, w_in, ln_g, ln_b, w_s, b_s_t, w_out, mg, mb)


def _mlp_kernel(x_ref, w1_ref, w2_ref, g_ref, b_ref, o_ref, *, ff_chunk):
    x = x_ref[...]
    x16 = x.astype(BF16)
    d_ff = w1_ref.shape[1]
    acc = None
    for c in range(d_ff // ff_chunk):
        cols = slice(c * ff_chunk, (c + 1) * ff_chunk)
        h = jnp.dot(x16, w1_ref[:, cols], preferred_element_type=F32)
        h = jnp.maximum(h, 0.0)
        h = (h * h).astype(BF16)
        part = jnp.dot(h, w2_ref[cols, :], preferred_element_type=F32)
        acc = part if acc is None else acc + part
    o_ref[...] = _layer_norm(ALPHA * x + acc, g_ref[...], b_ref[...])


def _mlp_layer(x2, w1, w2, g, b, *, tm, ff_chunk):
    n, d = x2.shape
    d_ff = w1.shape[1]
    row = pl.BlockSpec((tm, d), lambda i: (i, 0))
    return pl.pallas_call(
        functools.partial(_mlp_kernel, ff_chunk=ff_chunk),
        grid=(n // tm,),
        in_specs=[row, _const_spec((d, d_ff)), _const_spec((d_ff, d)),
                  _const_spec((1, d)), _const_spec((1, d))],
        out_specs=row,
        out_shape=jax.ShapeDtypeStruct((n, d), F32),
        compiler_params=pltpu.CompilerParams(
            dimension_semantics=("arbitrary",), vmem_limit_bytes=VMEM_LIMIT),
        name="relu2_mlp",
    )(x2, w1, w2, g, b)


def _kv_kernel(x_ref, wkt_ref, wv_ref, kt_ref, v2_ref):
    x16 = x_ref[0].astype(BF16)
    kt = lax.dot_general(wkt_ref[...], x16, (((1,), (1,)), ((), ())),
                         preferred_element_type=F32)
    kt_ref[0] = kt.astype(BF16)
    v = jnp.dot(x16, wv_ref[...], preferred_element_type=F32).astype(BF16)
    kb = SB_KEY_BLOCK
    n_pairs, n_blocks = v2_ref.shape[1], v2_ref.shape[2]
    first_head = lax.broadcasted_iota(jnp.int32, (kb, LANES), 1) < LANES // 2
    zero = jnp.zeros((kb, LANES), BF16)
    for p in range(n_pairs):
        for r in range(n_blocks):
            blk = v[r * kb:(r + 1) * kb, p * LANES:(p + 1) * LANES]
            v2_ref[0, p, r, :kb, :] = jnp.where(first_head, blk, zero)
            v2_ref[0, p, r, kb:, :] = jnp.where(first_head, zero, blk)


def _kv_proj(x3, wk_t, wv, *, tm):
    b, s, d = x3.shape
    kb = SB_KEY_BLOCK
    return pl.pallas_call(
        _kv_kernel,
        grid=(b, s // tm),
        in_specs=[pl.BlockSpec((1, tm, d), lambda i, j: (i, j, 0)),
                  _const_spec((d, d)), _const_spec((d, d))],
        out_specs=[pl.BlockSpec((1, d, tm), lambda i, j: (i, 0, j)),
                   pl.BlockSpec((1, d // LANES, tm // kb, 2 * kb, LANES),
                                lambda i, j: (i, 0, j, 0, 0))],
        out_shape=[jax.ShapeDtypeStruct((b, d, s), BF16),
                   jax.ShapeDtypeStruct((b, d // LANES, s // kb, 2 * kb, LANES), BF16)],
        compiler_params=pltpu.CompilerParams(
            dimension_semantics=("arbitrary", "arbitrary"), vmem_limit_bytes=VMEM_LIMIT),
        name="kv_proj",
    )(x3, wk_t, wv)


def _sb_kernel(x_ref, wq_ref, kt_ref, v2_ref, wo_ref, g_ref, b_ref, o_ref,
               q_ref, acc_ref, rem_ref, *, tq):
    d = x_ref.shape[2]
    dh = d // SB_HEADS
    assert 2 * dh == LANES and tq == LANES and tq == SB_KEY_BLOCK
    scale = dh ** -0.5 * LOG2E
    qi = pl.program_id(1)
    x = x_ref[0]
    q = jnp.dot(x.astype(BF16), wq_ref[...], preferred_element_type=F32)
    q_ref[...] = (q * scale).astype(BF16)

    row = lax.broadcasted_iota(jnp.int32, (tq, tq), 0)
    col = lax.broadcasted_iota(jnp.int32, (tq, tq), 1)
    causal = col < row
    r2 = lax.broadcasted_iota(jnp.int32, (2 * tq, 2 * tq), 0)
    c2 = lax.broadcasted_iota(jnp.int32, (2 * tq, 2 * tq), 1)
    tri2 = jnp.logical_or(c2 >= tq, (r2 & (tq - 1)) > c2).astype(BF16)
    kt_zeros = jnp.zeros((dh, tq), BF16)

    def key_block(kb, diag):
        k0 = pl.multiple_of(kb * tq, tq)
        heads = range(SB_HEADS)
        zs = []
        for h in heads:
            p, j = divmod(h, 2)
            q_p = q_ref[:, p * LANES:(p + 1) * LANES]
            kt_h = kt_ref[0, h * dh:(h + 1) * dh, pl.ds(k0, tq)]
            kt_h = jnp.concatenate([kt_h, kt_zeros] if j == 0 else [kt_zeros, kt_h], axis=0)
            zs.append(jnp.dot(q_p, kt_h, preferred_element_type=F32))
        log_betas, splits = [], []
        for h in heads:
            z = zs[h]
            sp = jnp.maximum(z, 0.0) + jnp.log2(1.0 + jnp.exp2(-jnp.abs(z)))
            log_betas.append(z - sp)
            if diag:
                sp = jnp.where(causal, sp, 0.0)
            hi = sp.astype(BF16)
            lo = (sp - hi.astype(F32)).astype(BF16)
            splits.append(jnp.concatenate([hi, lo], axis=1))
        sums = [jnp.dot(splits[h], tri2, preferred_element_type=F32) for h in heads]
        weights = []
        for h in heads:
            suffix, total = sums[h][:, :tq], sums[h][:, tq:]
            if diag:
                a = jnp.where(causal, jnp.exp2(log_betas[h] - suffix), 0.0)
                rem_ref[h] = total
            else:
                rem = rem_ref[h]
                a = jnp.exp2(log_betas[h] - (suffix + rem))
                rem_ref[h] = rem + total
            weights.append(a.astype(BF16))
        for p in range(SB_HEADS // 2):
            lanes = slice(p * LANES, (p + 1) * LANES)
            av = jnp.dot(jnp.concatenate(weights[2 * p:2 * p + 2], axis=1), v2_ref[0, p, kb],
                         preferred_element_type=F32)
            if diag:
                acc_ref[:, lanes] = av
            else:
                acc_ref[:, lanes] += av

    def unfinished():
        m = rem_ref[0]
        for h in range(1, SB_HEADS):
            m = jnp.minimum(m, rem_ref[h])
        return (jnp.min(m) < F32_EXP2_ZERO).astype(jnp.int32)

    key_block(qi, True)

    def body(carry):
        kb, _ = carry
        key_block(kb, False)
        return kb - 1, unfinished()

    def cond(carry):
        kb, go = carry
        return jnp.logical_and(kb >= 0, go > 0)

    lax.while_loop(cond, body, (qi - 1, unfinished()))

    mix = jnp.dot(acc_ref[...].astype(BF16), wo_ref[...], preferred_element_type=F32)
    o_ref[0] = _layer_norm(ALPHA * x + mix, g_ref[...], b_ref[...])


def _sb_layer(x3, wq, kt, v2, wo, g, b, *, tq):
    bsz, s, d = x3.shape
    row = pl.BlockSpec((1, tq, d), lambda i, j: (i, j, 0))
    return pl.pallas_call(
        functools.partial(_sb_kernel, tq=tq),
        grid=(bsz, s // tq),
        in_specs=[row, _const_spec((d, d)),
                  pl.BlockSpec((1, d, s), lambda i, j: (i, 0, 0), pipeline_mode=pl.Buffered(1)),
                  pl.BlockSpec((1,) + v2.shape[1:], lambda i, j: (i, 0, 0, 0, 0),
                               pipeline_mode=pl.Buffered(1)),
                  _const_spec((d, d)), _const_spec((1, d)), _const_spec((1, d))],
        out_specs=row,
        out_shape=jax.ShapeDtypeStruct((bsz, s, d), F32),
        scratch_shapes=[pltpu.VMEM((tq, d), BF16), pltpu.VMEM((tq, d), F32),
                        pltpu.VMEM((SB_HEADS, tq, LANES), F32)],
        compiler_params=pltpu.CompilerParams(
            dimension_semantics=("arbitrary", "arbitrary"), vmem_limit_bytes=VMEM_LIMIT),
        name="sb_mixer",
    )(x3, wq, kt, v2, wo, g, b)


def kernel(x, a_w_in, a_ln_g, a_ln_b, a_w_s, a_b_s, a_w_out, sb_w_k, sb_w_v, b_w_q, b_w_o,
           mix_ln_g, mix_ln_b, ffn_ln_g, ffn_ln_b, ffn_w1, ffn_w2):
    bsz, seq, d = x.shape
    n = bsz * seq
    vec = lambda p: p.reshape(1, -1).astype(F32)
    x2 = x.reshape(n, d)
    kt = v = None
    for l in range(DEPTH):
        if l < N_A:
            x2 = _gmlp_layer(
                x2, a_w_in[l].astype(BF16), vec(a_ln_g[l]), vec(a_ln_b[l]), a_w_s[l],
                jnp.transpose(a_b_s[l]), a_w_out[l].astype(BF16),
                vec(mix_ln_g[l]), vec(mix_ln_b[l]), tm=512)
        else:
            x3 = x2.reshape(bsz, seq, d)
            if l == N_A:
                kt, v = _kv_proj(x3, jnp.transpose(sb_w_k).astype(BF16),
                                 sb_w_v.astype(BF16), tm=512)
            j = l - N_A
            x2 = _sb_layer(x3, b_w_q[j].astype(BF16), kt, v, b_w_o[j].astype(BF16),
                           vec(mix_ln_g[l]), vec(mix_ln_b[l]), tq=128).reshape(n, d)
        x2 = _mlp_layer(x2, ffn_w1[l].astype(BF16), ffn_w2[l].astype(BF16),
                        vec(ffn_ln_g[l]), vec(ffn_ln_b[l]), tm=512, ff_chunk=1024)
    return x2.reshape(bsz, seq, d)
```

```python
import functools

import jax
import jax.numpy as jnp
from jax import lax
from jax.experimental import pallas as pl
from jax.experimental.pallas import tpu as pltpu

F32 = jnp.float32
BF16 = jnp.bfloat16

LANES = 128
CHUNK = 64
GMLP_BLOCK = 128
GMLP_GROUPS = 8
SB_HEADS = 16
LN_EPS = 1e-5
DEPTH = 4
N_A = DEPTH // 2
ALPHA = float((2 * DEPTH) ** 0.25)

LOG2E = 1.4426950408889634
F32_EXP2_ZERO = 127.0
SB_KEY_BLOCK = 128
SB_TOP_ROWS = 32

VMEM_LIMIT = 56 * 1024 * 1024


def _layer_norm(y, g, b):
    mu = jnp.mean(y, axis=-1, keepdims=True)
    d = y - mu
    var = jnp.mean(d * d, axis=-1, keepdims=True)
    return d * lax.rsqrt(var + LN_EPS) * g + b


def _gelu_tanh(x):
    c = 0.7978845608028654
    return 0.5 * x * (1.0 + jnp.tanh(c * (x + 0.044715 * (x * x * x))))


def _const_spec(shape):
    zeros = (0,) * len(shape)
    return pl.BlockSpec(shape, lambda *_: zeros, pipeline_mode=pl.Buffered(1))


def _gmlp_kernel(x_ref, win_ref, lng_ref, lnb_ref, ws_ref, bs_ref, wout_ref,
                 mg_ref, mb_ref, o_ref, gated_ref):
    tm, d = x_ref.shape
    w = wout_ref.shape[0]
    gd = w // GMLP_GROUPS
    x = x_ref[...]
    z = jnp.dot(x.astype(BF16), win_ref[...], preferred_element_type=F32)
    z = _gelu_tanh(z)
    u = z[:, :w]
    v = _layer_norm(z[:, w:], lng_ref[...], lnb_ref[...]).astype(BF16)
    t_idx = lax.broadcasted_iota(jnp.int32, (GMLP_BLOCK, GMLP_BLOCK), 0)
    s_idx = lax.broadcasted_iota(jnp.int32, (GMLP_BLOCK, GMLP_BLOCK), 1)
    allowed = (s_idx // CHUNK) <= (t_idx // CHUNK)
    for g in range(GMLP_GROUPS):
        ws = jnp.where(allowed, ws_ref[g], 0.0).astype(BF16)
        bias = bs_ref[:, g:g + 1]
        for r in range(tm // GMLP_BLOCK):
            rows = slice(r * GMLP_BLOCK, (r + 1) * GMLP_BLOCK)
            cols = slice(g * gd, (g + 1) * gd)
            s = jnp.dot(ws, v[rows, cols], preferred_element_type=F32) + bias
            gated_ref[rows, cols] = (u[rows, cols] * s).astype(BF16)
    mix = jnp.dot(gated_ref[...], wout_ref[...], preferred_element_type=F32)
    o_ref[...] = _layer_norm(ALPHA * x + mix, mg_ref[...], mb_ref[...])


def _gmlp_layer(x2, w_in, ln_g, ln_b, w_s, b_s_t, w_out, mg, mb, *, tm):
    n, d = x2.shape
    w = w_out.shape[0]
    row = pl.BlockSpec((tm, d), lambda i: (i, 0))
    return pl.pallas_call(
        _gmlp_kernel,
        grid=(n // tm,),
        in_specs=[row, _const_spec((d, 2 * w)), _const_spec((1, w)), _const_spec((1, w)),
                  _const_spec(w_s.shape), _const_spec(b_s_t.shape), _const_spec((w, d)),
                  _const_spec((1, d)), _const_spec((1, d))],
        out_specs=row,
        out_shape=jax.ShapeDtypeStruct((n, d), F32),
        scratch_shapes=[pltpu.VMEM((tm, w), BF16)],
        compiler_params=pltpu.CompilerParams(
            dimension_semantics=("arbitrary",), vmem_limit_bytes=VMEM_LIMIT),
        name="gmlp_mixer",
    )(x2, w_in, ln_g, ln_b, w_s, b_s_t, w_out, mg, mb)


def _mlp_kernel(x_ref, w1_ref, w2_ref, g_ref, b_ref, o_ref, *, ff_chunk):
    x = x_ref[...]
    x16 = x.astype(BF16)
    d_ff = w1_ref.shape[1]
    acc = None
    for c in range(d_ff // ff_chunk):
        cols = slice(c * ff_chunk, (c + 1) * ff_chunk)
        h = jnp.dot(x16, w1_ref[:, cols], preferred_element_type=F32)
        h = jnp.maximum(h, 0.0)
        h = (h * h).astype(BF16)
        part = jnp.dot(h, w2_ref[cols, :], preferred_element_type=F32)
        acc = part if acc is None else acc + part
    o_ref[...] = _layer_norm(ALPHA * x + acc, g_ref[...], b_ref[...])


def _mlp_layer(x2, w1, w2, g, b, *, tm, ff_chunk):
    n, d = x2.shape
    d_ff = w1.shape[1]
    row = pl.BlockSpec((tm, d), lambda i: (i, 0))
    return pl.pallas_call(
        functools.partial(_mlp_kernel, ff_chunk=ff_chunk),
        grid=(n // tm,),
        in_specs=[row, _const_spec((d, d_ff)), _const_spec((d_ff, d)),
                  _const_spec((1, d)), _const_spec((1, d))],
        out_specs=row,
        out_shape=jax.ShapeDtypeStruct((n, d), F32),
        compiler_params=pltpu.CompilerParams(
            dimension_semantics=("arbitrary",), vmem_limit_bytes=VMEM_LIMIT),
        name="relu2_mlp",
    )(x2, w1, w2, g, b)


def _kv_kernel(x_ref, wkt_ref, wv_ref, kt_ref, v2_ref):
    x16 = x_ref[0].astype(BF16)
    kt = lax.dot_general(wkt_ref[...], x16, (((1,), (1,)), ((), ())),
                         preferred_element_type=F32)
    kt_ref[0] = kt.astype(BF16)
    v = jnp.dot(x16, wv_ref[...], preferred_element_type=F32).astype(BF16)
    kb = SB_KEY_BLOCK
    n_pairs, n_blocks = v2_ref.shape[1], v2_ref.shape[2]
    first_head = lax.broadcasted_iota(jnp.int32, (kb, LANES), 1) < LANES // 2
    zero = jnp.zeros((kb, LANES), BF16)
    for p in range(n_pairs):
        for r in range(n_blocks):
            blk = v[r * kb:(r + 1) * kb, p * LANES:(p + 1) * LANES]
            v2_ref[0, p, r, :kb, :] = jnp.where(first_head, blk, zero)
            v2_ref[0, p, r, kb:, :] = jnp.where(first_head, zero, blk)


def _kv_proj(x3, wk_t, wv, *, tm):
    b, s, d = x3.shape
    kb = SB_KEY_BLOCK
    return pl.pallas_call(
        _kv_kernel,
        grid=(b, s // tm),
        in_specs=[pl.BlockSpec((1, tm, d), lambda i, j: (i, j, 0)),
                  _const_spec((d, d)), _const_spec((d, d))],
        out_specs=[pl.BlockSpec((1, d, tm), lambda i, j: (i, 0, j)),
                   pl.BlockSpec((1, d // LANES, tm // kb, 2 * kb, LANES),
                                lambda i, j: (i, 0, j, 0, 0))],
        out_shape=[jax.ShapeDtypeStruct((b, d, s), BF16),
                   jax.ShapeDtypeStruct((b, d // LANES, s // kb, 2 * kb, LANES), BF16)],
        compiler_params=pltpu.CompilerParams(
            dimension_semantics=("arbitrary", "arbitrary"), vmem_limit_bytes=VMEM_LIMIT),
        name="kv_proj",
    )(x3, wk_t, wv)


def _sb_kernel(x_ref, wq_ref, kt_ref, v2_ref, wo_ref, g_ref, b_ref, o_ref,
               q_ref, acc_ref, rem_ref, *, tq):
    d = x_ref.shape[2]
    dh = d // SB_HEADS
    ts = SB_KEY_BLOCK
    nt = SB_TOP_ROWS
    assert 2 * dh == LANES and ts == LANES and tq % ts == 0
    scale = dh ** -0.5 * LOG2E
    qi = pl.program_id(1)
    x = x_ref[0]
    q = jnp.dot(x.astype(BF16), wq_ref[...], preferred_element_type=F32)
    q_ref[...] = (q * scale).astype(BF16)

    row = lax.broadcasted_iota(jnp.int32, (ts, ts), 0)
    col = lax.broadcasted_iota(jnp.int32, (ts, ts), 1)
    causal = col < row
    r2 = lax.broadcasted_iota(jnp.int32, (2 * ts, 2 * ts), 0)
    c2 = lax.broadcasted_iota(jnp.int32, (2 * ts, 2 * ts), 1)
    tri2 = jnp.logical_or(c2 >= ts, (r2 & (ts - 1)) > c2).astype(BF16)
    kt_zeros = jnp.zeros((dh, ts), BF16)
    heads = range(SB_HEADS)

    def logits(rows, h, kb):
        p, j = divmod(h, 2)
        q_p = q_ref[rows, p * LANES:(p + 1) * LANES]
        kt_h = kt_ref[0, h * dh:(h + 1) * dh, pl.ds(pl.multiple_of(kb * ts, ts), ts)]
        kt_h = jnp.concatenate([kt_h, kt_zeros] if j == 0 else [kt_zeros, kt_h], axis=0)
        return jnp.dot(q_p, kt_h, preferred_element_type=F32)

    def softplus2(z):
        return jnp.maximum(z, 0.0) + jnp.log2(1.0 + jnp.exp2(-jnp.abs(z)))

    def split(sp):
        hi = sp.astype(BF16)
        lo = (sp - hi.astype(F32)).astype(BF16)
        return jnp.concatenate([hi, lo], axis=1)

    def diag_step(r0, qb):
        all_rows, top_rows = pl.ds(r0, ts), pl.ds(r0, nt)
        kx = jnp.maximum(qb - 1, 0)
        has_x = qb >= 1
        z_d = [logits(all_rows, h, qb) for h in heads]
        z_x = [logits(top_rows, h, kx) for h in heads]
        lb_d, lb_x, splits = [], [], []
        for h in heads:
            sp_d = softplus2(z_d[h])
            sp_x = softplus2(z_x[h])
            lb_d.append(z_d[h] - sp_d)
            lb_x.append(z_x[h] - sp_x)
            sp_d = jnp.where(causal, sp_d, 0.0)
            sp_x = jnp.where(has_x, sp_x, 0.0)
            splits.append(jnp.concatenate([split(sp_d), split(sp_x)], axis=0))
        sums = [jnp.dot(splits[h], tri2, preferred_element_type=F32) for h in heads]
        w_d, w_x = [], []
        for h in heads:
            suffix_d, total_d = sums[h][:ts, :ts], sums[h][:ts, ts:]
            suffix_x, total_x = sums[h][ts:, :ts], sums[h][ts:, ts:]
            a_d = jnp.where(causal, jnp.exp2(lb_d[h] - suffix_d), 0.0)
            a_x = jnp.exp2(lb_x[h] - (suffix_x + total_d[:nt]))
            a_x = jnp.where(has_x, a_x, 0.0)
            rem_ref[h, :nt] = total_d[:nt] + total_x
            rem_ref[h, nt:] = total_d[nt:]
            w_d.append(a_d.astype(BF16))
            w_x.append(a_x.astype(BF16))
        for p in range(SB_HEADS // 2):
            lanes = slice(p * LANES, (p + 1) * LANES)
            av_d = jnp.dot(jnp.concatenate(w_d[2 * p:2 * p + 2], axis=1), v2_ref[0, p, qb],
                           preferred_element_type=F32)
            av_x = jnp.dot(jnp.concatenate(w_x[2 * p:2 * p + 2], axis=1), v2_ref[0, p, kx],
                           preferred_element_type=F32)
            acc_ref[top_rows, lanes] = av_d[:nt] + av_x
            acc_ref[pl.ds(r0 + nt, ts - nt), lanes] = av_d[nt:]

    def loop_step(r0, kb):
        top_rows, rest_rows = pl.ds(r0, nt), pl.ds(r0 + nt, ts - nt)
        kx = jnp.maximum(kb - 1, 0)
        has_x = kb >= 1
        zs = [jnp.concatenate([logits(top_rows, h, kx), logits(rest_rows, h, kb)], axis=0)
              for h in heads]
        lbs, splits = [], []
        for h in heads:
            sp = softplus2(zs[h])
            lbs.append(zs[h] - sp)
            sp = jnp.concatenate([jnp.where(has_x, sp[:nt], 0.0), sp[nt:]], axis=0)
            splits.append(split(sp))
        sums = [jnp.dot(splits[h], tri2, preferred_element_type=F32) for h in heads]
        weights = []
        for h in heads:
            suffix, total = sums[h][:, :ts], sums[h][:, ts:]
            rem = rem_ref[h]
            a = jnp.exp2(lbs[h] - (suffix + rem))
            a = jnp.concatenate([jnp.where(has_x, a[:nt], 0.0), a[nt:]], axis=0)
            rem_ref[h] = rem + total
            weights.append(a.astype(BF16))
        for p in range(SB_HEADS // 2):
            lanes = slice(p * LANES, (p + 1) * LANES)
            a_p = jnp.concatenate(weights[2 * p:2 * p + 2], axis=1)
            acc_ref[top_rows, lanes] += jnp.dot(a_p[:nt], v2_ref[0, p, kx],
                                                preferred_element_type=F32)
            acc_ref[rest_rows, lanes] += jnp.dot(a_p[nt:], v2_ref[0, p, kb],
                                                 preferred_element_type=F32)

    def unfinished():
        m = rem_ref[0]
        for h in range(1, SB_HEADS):
            m = jnp.minimum(m, rem_ref[h])
        return (jnp.min(m) < F32_EXP2_ZERO).astype(jnp.int32)

    def sub_tile(sub, carry):
        r0 = pl.multiple_of(sub * ts, ts)
        qb = qi * (tq // ts) + sub
        diag_step(r0, qb)

        def body(c):
            kb, _ = c
            loop_step(r0, kb)
            return kb - 1, unfinished()

        def cond(c):
            kb, go = c
            return jnp.logical_and(kb >= 0, go > 0)

        lax.while_loop(cond, body, (qb - 1, unfinished()))
        return carry

    lax.fori_loop(0, tq // ts, sub_tile, 0)

    mix = jnp.dot(acc_ref[...].astype(BF16), wo_ref[...], preferred_element_type=F32)
    o_ref[0] = _layer_norm(ALPHA * x + mix, g_ref[...], b_ref[...])


def _sb_layer(x3, wq, kt, v2, wo, g, b, *, tq):
    bsz, s, d = x3.shape
    row = pl.BlockSpec((1, tq, d), lambda i, j: (i, j, 0))
    return pl.pallas_call(
        functools.partial(_sb_kernel, tq=tq),
        grid=(bsz, s // tq),
        in_specs=[row, _const_spec((d, d)),
                  pl.BlockSpec((1, d, s), lambda i, j: (i, 0, 0), pipeline_mode=pl.Buffered(1)),
                  pl.BlockSpec((1,) + v2.shape[1:], lambda i, j: (i, 0, 0, 0, 0),
                               pipeline_mode=pl.Buffered(1)),
                  _const_spec((d, d)), _const_spec((1, d)), _const_spec((1, d))],
        out_specs=row,
        out_shape=jax.ShapeDtypeStruct((bsz, s, d), F32),
        scratch_shapes=[pltpu.VMEM((tq, d), BF16), pltpu.VMEM((tq, d), F32),
                        pltpu.VMEM((SB_HEADS, SB_KEY_BLOCK, LANES), F32)],
        compiler_params=pltpu.CompilerParams(
            dimension_semantics=("arbitrary", "arbitrary"), vmem_limit_bytes=VMEM_LIMIT),
        name="sb_mixer",
    )(x3, wq, kt, v2, wo, g, b)


def kernel(x, a_w_in, a_ln_g, a_ln_b, a_w_s, a_b_s, a_w_out, sb_w_k, sb_w_v, b_w_q, b_w_o,
           mix_ln_g, mix_ln_b, ffn_ln_g, ffn_ln_b, ffn_w1, ffn_w2):
    bsz, seq, d = x.shape
    n = bsz * seq
    vec = lambda p: p.reshape(1, -1).astype(F32)
    x2 = x.reshape(n, d)
    kt = v = None
    for l in range(DEPTH):
        if l < N_A:
            x2 = _gmlp_layer(
                x2, a_w_in[l].astype(BF16), vec(a_ln_g[l]), vec(a_ln_b[l]), a_w_s[l],
                jnp.transpose(a_b_s[l]), a_w_out[l].astype(BF16),
                vec(mix_ln_g[l]), vec(mix_ln_b[l]), tm=512)
        else:
            x3 = x2.reshape(bsz, seq, d)
            if l == N_A:
                kt, v = _kv_proj(x3, jnp.transpose(sb_w_k).astype(BF16),
                                 sb_w_v.astype(BF16), tm=512)
            j = l - N_A
            x2 = _sb_layer(x3, b_w_q[j].astype(BF16), kt, v, b_w_o[j].astype(BF16),
                           vec(mix_ln_g[l]), vec(mix_ln_b[l]), tq=256).reshape(n, d)
        x2 = _mlp_layer(x2, ffn_w1[l].astype(BF16), ffn_w2[l].astype(BF16),
                        vec(ffn_ln_g[l]), vec(ffn_ln_b[l]), tm=512, ff_chunk=1024)
    return x2.reshape(bsz, seq, d)
```

```python
import functools

import jax
import jax.numpy as jnp
from jax import lax
from jax.experimental import pallas as pl
from jax.experimental.pallas import tpu as pltpu

F32 = jnp.float32
BF16 = jnp.bfloat16

LANES = 128
CHUNK = 64
GMLP_BLOCK = 128
GMLP_GROUPS = 8
SB_HEADS = 16
LN_EPS = 1e-5
DEPTH = 4
N_A = DEPTH // 2
ALPHA = float((2 * DEPTH) ** 0.25)

LOG2E = 1.4426950408889634
F32_EXP2_ZERO = 127.0
SB_KEY_BLOCK = 128
SB_TOP_ROWS = 32

VMEM_LIMIT = 56 * 1024 * 1024


def _layer_norm(y, g, b):
    mu = jnp.mean(y, axis=-1, keepdims=True)
    d = y - mu
    var = jnp.mean(d * d, axis=-1, keepdims=True)
    return d * lax.rsqrt(var + LN_EPS) * g + b


def _gelu_tanh(x):
    c = 0.7978845608028654
    return 0.5 * x * (1.0 + jnp.tanh(c * (x + 0.044715 * (x * x * x))))


def _const_spec(shape):
    zeros = (0,) * len(shape)
    return pl.BlockSpec(shape, lambda *_: zeros, pipeline_mode=pl.Buffered(1))


def _gmlp_kernel(x_ref, win_ref, lng_ref, lnb_ref, ws_ref, bs_ref, wout_ref,
                 mg_ref, mb_ref, o_ref, gated_ref):
    tm, d = x_ref.shape
    w = wout_ref.shape[0]
    gd = w // GMLP_GROUPS
    x = x_ref[...]
    z = jnp.dot(x.astype(BF16), win_ref[...], preferred_element_type=F32)
    z = _gelu_tanh(z)
    u = z[:, :w]
    v = _layer_norm(z[:, w:], lng_ref[...], lnb_ref[...]).astype(BF16)
    t_idx = lax.broadcasted_iota(jnp.int32, (GMLP_BLOCK, GMLP_BLOCK), 0)
    s_idx = lax.broadcasted_iota(jnp.int32, (GMLP_BLOCK, GMLP_BLOCK), 1)
    allowed = (s_idx // CHUNK) <= (t_idx // CHUNK)
    for g in range(GMLP_GROUPS):
        ws = jnp.where(allowed, ws_ref[g], 0.0).astype(BF16)
        bias = bs_ref[:, g:g + 1]
        for r in range(tm // GMLP_BLOCK):
            rows = slice(r * GMLP_BLOCK, (r + 1) * GMLP_BLOCK)
            cols = slice(g * gd, (g + 1) * gd)
            s = jnp.dot(ws, v[rows, cols], preferred_element_type=F32) + bias
            gated_ref[rows, cols] = (u[rows, cols] * s).astype(BF16)
    mix = jnp.dot(gated_ref[...], wout_ref[...], preferred_element_type=F32)
    o_ref[...] = _layer_norm(ALPHA * x + mix, mg_ref[...], mb_ref[...])


def _gmlp_layer(x2, w_in, ln_g, ln_b, w_s, b_s_t, w_out, mg, mb, *, tm):
    n, d = x2.shape
    w = w_out.shape[0]
    row = pl.BlockSpec((tm, d), lambda i: (i, 0))
    return pl.pallas_call(
        _gmlp_kernel,
        grid=(n // tm,),
        in_specs=[row, _const_spec((d, 2 * w)), _const_spec((1, w)), _const_spec((1, w)),
                  _const_spec(w_s.shape), _const_spec(b_s_t.shape), _const_spec((w, d)),
                  _const_spec((1, d)), _const_spec((1, d))],
        out_specs=row,
        out_shape=jax.ShapeDtypeStruct((n, d), F32),
        scratch_shapes=[pltpu.VMEM((tm, w), BF16)],
        compiler_params=pltpu.CompilerParams(
            dimension_semantics=("arbitrary",), vmem_limit_bytes=VMEM_LIMIT),
        name="gmlp_mixer",
    )(x2, w_in, ln_g, ln_b, w_s, b_s_t, w_out, mg, mb)


def _mlp_kernel(x_ref, w1_ref, w2_ref, g_ref, b_ref, o_ref, *, ff_chunk):
    x = x_ref[...]
    x16 = x.astype(BF16)
    d_ff = w1_ref.shape[1]
    acc = None
    for c in range(d_ff // ff_chunk):
        cols = slice(c * ff_chunk, (c + 1) * ff_chunk)
        h = jnp.dot(x16, w1_ref[:, cols], preferred_element_type=F32)
        h = jnp.maximum(h, 0.0)
        h = (h * h).astype(BF16)
        part = jnp.dot(h, w2_ref[cols, :], preferred_element_type=F32)
        acc = part if acc is None else acc + part
    o_ref[...] = _layer_norm(ALPHA * x + acc, g_ref[...], b_ref[...])


def _mlp_layer(x2, w1, w2, g, b, *, tm, ff_chunk):
    n, d = x2.shape
    d_ff = w1.shape[1]
    row = pl.BlockSpec((tm, d), lambda i: (i, 0))
    return pl.pallas_call(
        functools.partial(_mlp_kernel, ff_chunk=ff_chunk),
        grid=(n // tm,),
        in_specs=[row, _const_spec((d, d_ff)), _const_spec((d_ff, d)),
                  _const_spec((1, d)), _const_spec((1, d))],
        out_specs=row,
        out_shape=jax.ShapeDtypeStruct((n, d), F32),
        compiler_params=pltpu.CompilerParams(
            dimension_semantics=("arbitrary",), vmem_limit_bytes=VMEM_LIMIT),
        name="relu2_mlp",
    )(x2, w1, w2, g, b)


def _kv_kernel(x_ref, wkt_ref, wv_ref, kt_ref, v2_ref):
    x16 = x_ref[0].astype(BF16)
    kt = lax.dot_general(wkt_ref[...], x16, (((1,), (1,)), ((), ())),
                         preferred_element_type=F32)
    kt_ref[0] = kt.astype(BF16)
    v = jnp.dot(x16, wv_ref[...], preferred_element_type=F32).astype(BF16)
    kb = SB_KEY_BLOCK
    n_pairs, n_blocks = v2_ref.shape[1], v2_ref.shape[2]
    first_head = lax.broadcasted_iota(jnp.int32, (kb, LANES), 1) < LANES // 2
    zero = jnp.zeros((kb, LANES), BF16)
    for p in range(n_pairs):
        for r in range(n_blocks):
            blk = v[r * kb:(r + 1) * kb, p * LANES:(p + 1) * LANES]
            v2_ref[0, p, r, :kb, :] = jnp.where(first_head, blk, zero)
            v2_ref[0, p, r, kb:, :] = jnp.where(first_head, zero, blk)


def _kv_proj(x3, wk_t, wv, *, tm):
    b, s, d = x3.shape
    kb = SB_KEY_BLOCK
    return pl.pallas_call(
        _kv_kernel,
        grid=(b, s // tm),
        in_specs=[pl.BlockSpec((1, tm, d), lambda i, j: (i, j, 0)),
                  _const_spec((d, d)), _const_spec((d, d))],
        out_specs=[pl.BlockSpec((1, d, tm), lambda i, j: (i, 0, j)),
                   pl.BlockSpec((1, d // LANES, tm // kb, 2 * kb, LANES),
                                lambda i, j: (i, 0, j, 0, 0))],
        out_shape=[jax.ShapeDtypeStruct((b, d, s), BF16),
                   jax.ShapeDtypeStruct((b, d // LANES, s // kb, 2 * kb, LANES), BF16)],
        compiler_params=pltpu.CompilerParams(
            dimension_semantics=("arbitrary", "arbitrary"), vmem_limit_bytes=VMEM_LIMIT),
        name="kv_proj",
    )(x3, wk_t, wv)


def _sb_kernel(x_ref, wq_ref, kt_ref, v2_ref, wo_ref, g_ref, b_ref, o_ref,
               q_ref, acc_ref, rem_ref, *, tq):
    d = x_ref.shape[2]
    dh = d // SB_HEADS
    ts = SB_KEY_BLOCK
    nt = SB_TOP_ROWS
    assert 2 * dh == LANES and ts == LANES and tq % ts == 0
    scale = dh ** -0.5 * LOG2E
    qi = pl.program_id(1)
    x = x_ref[0]
    q = jnp.dot(x.astype(BF16), wq_ref[...], preferred_element_type=F32)
    q_ref[...] = (q * scale).astype(BF16)

    row = lax.broadcasted_iota(jnp.int32, (ts, ts), 0)
    col = lax.broadcasted_iota(jnp.int32, (ts, ts), 1)
    causal = col < row
    r2 = lax.broadcasted_iota(jnp.int32, (ts, 2 * ts), 0)
    c2 = lax.broadcasted_iota(jnp.int32, (ts, 2 * ts), 1)
    tri2 = jnp.logical_or(c2 >= ts, r2 > c2).astype(BF16)
    kt_zeros = jnp.zeros((dh, ts), BF16)
    heads = range(SB_HEADS)

    def logits(rows, h, kb):
        p, j = divmod(h, 2)
        q_p = q_ref[rows, p * LANES:(p + 1) * LANES]
        kt_h = kt_ref[0, h * dh:(h + 1) * dh, pl.ds(pl.multiple_of(kb * ts, ts), ts)]
        kt_h = jnp.concatenate([kt_h, kt_zeros] if j == 0 else [kt_zeros, kt_h], axis=0)
        return jnp.dot(q_p, kt_h, preferred_element_type=F32)

    def softplus2(z):
        return jnp.maximum(z, 0.0) + jnp.log2(1.0 + jnp.exp2(-jnp.abs(z)))

    def split(sp):
        return sp.astype(BF16)

    def diag_step(r0, qb):
        all_rows, top_rows = pl.ds(r0, ts), pl.ds(r0, nt)
        kx = jnp.maximum(qb - 1, 0)
        has_x = qb >= 1
        z_d = [logits(all_rows, h, qb) for h in heads]
        z_x = [logits(top_rows, h, kx) for h in heads]
        lb_d, lb_x, splits = [], [], []
        for h in heads:
            sp_d = softplus2(z_d[h])
            sp_x = softplus2(z_x[h])
            lb_d.append(z_d[h] - sp_d)
            lb_x.append(z_x[h] - sp_x)
            sp_d = jnp.where(causal, sp_d, 0.0)
            sp_x = jnp.where(has_x, sp_x, 0.0)
            splits.append(jnp.concatenate([split(sp_d), split(sp_x)], axis=0))
        sums = [jnp.dot(splits[h], tri2, preferred_element_type=F32) for h in heads]
        w_d, w_x = [], []
        for h in heads:
            suffix_d, total_d = sums[h][:ts, :ts], sums[h][:ts, ts:]
            suffix_x, total_x = sums[h][ts:, :ts], sums[h][ts:, ts:]
            a_d = jnp.where(causal, jnp.exp2(lb_d[h] - suffix_d), 0.0)
            a_x = jnp.exp2(lb_x[h] - (suffix_x + total_d[:nt]))
            a_x = jnp.where(has_x, a_x, 0.0)
            rem_ref[h, :nt] = total_d[:nt] + total_x
            rem_ref[h, nt:] = total_d[nt:]
            w_d.append(a_d.astype(BF16))
            w_x.append(a_x.astype(BF16))
        for p in range(SB_HEADS // 2):
            lanes = slice(p * LANES, (p + 1) * LANES)
            av_d = jnp.dot(jnp.concatenate(w_d[2 * p:2 * p + 2], axis=1), v2_ref[0, p, qb],
                           preferred_element_type=F32)
            av_x = jnp.dot(jnp.concatenate(w_x[2 * p:2 * p + 2], axis=1), v2_ref[0, p, kx],
                           preferred_element_type=F32)
            acc_ref[top_rows, lanes] = av_d[:nt] + av_x
            acc_ref[pl.ds(r0 + nt, ts - nt), lanes] = av_d[nt:]

    def loop_step(r0, kb):
        top_rows, rest_rows = pl.ds(r0, nt), pl.ds(r0 + nt, ts - nt)
        kx = jnp.maximum(kb - 1, 0)
        has_x = kb >= 1
        zs = [jnp.concatenate([logits(top_rows, h, kx), logits(rest_rows, h, kb)], axis=0)
              for h in heads]
        lbs, splits = [], []
        for h in heads:
            sp = softplus2(zs[h])
            lbs.append(zs[h] - sp)
            sp = jnp.concatenate([jnp.where(has_x, sp[:nt], 0.0), sp[nt:]], axis=0)
            splits.append(split(sp))
        sums = [jnp.dot(splits[h], tri2, preferred_element_type=F32) for h in heads]
        weights = []
        for h in heads:
            suffix, total = sums[h][:, :ts], sums[h][:, ts:]
            rem = rem_ref[h]
            a = jnp.exp2(lbs[h] - (suffix + rem))
            a = jnp.concatenate([jnp.where(has_x, a[:nt], 0.0), a[nt:]], axis=0)
            rem_ref[h] = rem + total
            weights.append(a.astype(BF16))
        for p in range(SB_HEADS // 2):
            lanes = slice(p * LANES, (p + 1) * LANES)
            a_p = jnp.concatenate(weights[2 * p:2 * p + 2], axis=1)
            acc_ref[top_rows, lanes] += jnp.dot(a_p[:nt], v2_ref[0, p, kx],
                                                preferred_element_type=F32)
            acc_ref[rest_rows, lanes] += jnp.dot(a_p[nt:], v2_ref[0, p, kb],
                                                 preferred_element_type=F32)

    def unfinished():
        m = rem_ref[0]
        for h in range(1, SB_HEADS):
            m = jnp.minimum(m, rem_ref[h])
        return (jnp.min(m) < F32_EXP2_ZERO).astype(jnp.int32)

    def sub_tile(sub, carry):
        r0 = pl.multiple_of(sub * ts, ts)
        qb = qi * (tq // ts) + sub
        diag_step(r0, qb)

        def body(c):
            kb, _ = c
            loop_step(r0, kb)
            return kb - 1, unfinished()

        def cond(c):
            kb, go = c
            return jnp.logical_and(kb >= 0, go > 0)

        lax.while_loop(cond, body, (qb - 1, jnp.int32(1)))
        return carry

    lax.fori_loop(0, tq // ts, sub_tile, 0)

    mix = jnp.dot(acc_ref[...].astype(BF16), wo_ref[...], preferred_element_type=F32)
    o_ref[0] = _layer_norm(ALPHA * x + mix, g_ref[...], b_ref[...])


def _sb_layer(x3, wq, kt, v2, wo, g, b, *, tq):
    bsz, s, d = x3.shape
    row = pl.BlockSpec((1, tq, d), lambda i, j: (i, j, 0))
    return pl.pallas_call(
        functools.partial(_sb_kernel, tq=tq),
        grid=(bsz, s // tq),
        in_specs=[row, _const_spec((d, d)),
                  pl.BlockSpec((1, d, s), lambda i, j: (i, 0, 0)),
                  pl.BlockSpec((1,) + v2.shape[1:], lambda i, j: (i, 0, 0, 0, 0),
                               pipeline_mode=pl.Buffered(1)),
                  _const_spec((d, d)), _const_spec((1, d)), _const_spec((1, d))],
        out_specs=row,
        out_shape=jax.ShapeDtypeStruct((bsz, s, d), F32),
        scratch_shapes=[pltpu.VMEM((tq, d), BF16), pltpu.VMEM((tq, d), F32),
                        pltpu.VMEM((SB_HEADS, SB_KEY_BLOCK, LANES), F32)],
        compiler_params=pltpu.CompilerParams(
            dimension_semantics=("arbitrary", "arbitrary"), vmem_limit_bytes=VMEM_LIMIT),
        name="sb_mixer",
    )(x3, wq, kt, v2, wo, g, b)


def kernel(x, a_w_in, a_ln_g, a_ln_b, a_w_s, a_b_s, a_w_out, sb_w_k, sb_w_v, b_w_q, b_w_o,
           mix_ln_g, mix_ln_b, ffn_ln_g, ffn_ln_b, ffn_w1, ffn_w2):
    bsz, seq, d = x.shape
    n = bsz * seq
    vec = lambda p: p.reshape(1, -1).astype(F32)
    x2 = x.reshape(n, d)
    kt = v = None
    for l in range(DEPTH):
        if l < N_A:
            x2 = _gmlp_layer(
                x2, a_w_in[l].astype(BF16), vec(a_ln_g[l]), vec(a_ln_b[l]), a_w_s[l],
                jnp.transpose(a_b_s[l]), a_w_out[l].astype(BF16),
                vec(mix_ln_g[l]), vec(mix_ln_b[l]), tm=512)
        else:
            x3 = x2.reshape(bsz, seq, d)
            if l == N_A:
                kt, v = _kv_proj(x3, jnp.transpose(sb_w_k).astype(BF16),
                                 sb_w_v.astype(BF16), tm=512)
            j = l - N_A
            x2 = _sb_layer(x3, b_w_q[j].astype(BF16), kt, v, b_w_o[j].astype(BF16),
                           vec(mix_ln_g[l]), vec(mix_ln_b[l]), tq=256).reshape(n, d)
        x2 = _mlp_layer(x2, ffn_w1[l].astype(BF16), ffn_w2[l].astype(BF16),
                        vec(ffn_ln_g[l]), vec(ffn_ln_b[l]), tm=512, ff_chunk=1024)
    return x2.reshape(bsz, seq, d)
```

```python
import functools

import jax
import jax.numpy as jnp
from jax import lax
from jax.experimental import pallas as pl
from jax.experimental.pallas import tpu as pltpu

F32 = jnp.float32
BF16 = jnp.bfloat16

LANES = 128
CHUNK = 64
GMLP_BLOCK = 128
GMLP_GROUPS = 8
SB_HEADS = 16
LN_EPS = 1e-5
DEPTH = 4
N_A = DEPTH // 2
ALPHA = float((2 * DEPTH) ** 0.25)

LOG2E = 1.4426950408889634
F32_EXP2_ZERO = 127.0
SB_KEY_BLOCK = 128
SB_TOP_ROWS = 32

VMEM_LIMIT = 56 * 1024 * 1024


def _layer_norm(y, g, b):
    mu = jnp.mean(y, axis=-1, keepdims=True)
    d = y - mu
    var = jnp.mean(d * d, axis=-1, keepdims=True)
    return d * lax.rsqrt(var + LN_EPS) * g + b


def _gelu_tanh(x):
    c = 0.7978845608028654
    return 0.5 * x * (1.0 + jnp.tanh(c * (x + 0.044715 * (x * x * x))))


def _const_spec(shape):
    zeros = (0,) * len(shape)
    return pl.BlockSpec(shape, lambda *_: zeros, pipeline_mode=pl.Buffered(1))


def _gmlp_kernel(x_ref, win_ref, lng_ref, lnb_ref, ws_ref, bs_ref, wout_ref,
                 mg_ref, mb_ref, o_ref, gated_ref):
    tm, d = x_ref.shape
    w = wout_ref.shape[0]
    gd = w // GMLP_GROUPS
    x16 = x_ref[...].astype(BF16)
    v = _gelu_tanh(jnp.dot(x16, win_ref[:, w:], preferred_element_type=F32))
    u = _gelu_tanh(jnp.dot(x16, win_ref[:, :w], preferred_element_type=F32))
    v = _layer_norm(v, lng_ref[...], lnb_ref[...]).astype(BF16)
    t_idx = lax.broadcasted_iota(jnp.int32, (GMLP_BLOCK, GMLP_BLOCK), 0)
    s_idx = lax.broadcasted_iota(jnp.int32, (GMLP_BLOCK, GMLP_BLOCK), 1)
    allowed = (s_idx // CHUNK) <= (t_idx // CHUNK)
    for g in range(GMLP_GROUPS):
        ws = jnp.where(allowed, ws_ref[g], 0.0).astype(BF16)
        bias = bs_ref[:, g:g + 1]
        for r in range(tm // GMLP_BLOCK):
            rows = slice(r * GMLP_BLOCK, (r + 1) * GMLP_BLOCK)
            cols = slice(g * gd, (g + 1) * gd)
            s = jnp.dot(ws, v[rows, cols], preferred_element_type=F32) + bias
            gated_ref[rows, cols] = (u[rows, cols] * s).astype(BF16)
    half = tm // 2
    for r in range(2):
        rows = slice(r * half, (r + 1) * half)
        mix = jnp.dot(gated_ref[rows, :], wout_ref[...], preferred_element_type=F32)
        o_ref[rows, :] = _layer_norm(ALPHA * x_ref[rows, :] + mix, mg_ref[...], mb_ref[...])


def _gmlp_layer(x2, w_in, ln_g, ln_b, w_s, b_s_t, w_out, mg, mb, *, tm):
    n, d = x2.shape
    w = w_out.shape[0]
    row = pl.BlockSpec((tm, d), lambda i: (i, 0))
    return pl.pallas_call(
        _gmlp_kernel,
        grid=(n // tm,),
        in_specs=[row, _const_spec((d, 2 * w)), _const_spec((1, w)), _const_spec((1, w)),
                  _const_spec(w_s.shape), _const_spec(b_s_t.shape), _const_spec((w, d)),
                  _const_spec((1, d)), _const_spec((1, d))],
        out_specs=row,
        out_shape=jax.ShapeDtypeStruct((n, d), F32),
        scratch_shapes=[pltpu.VMEM((tm, w), BF16)],
        compiler_params=pltpu.CompilerParams(
            dimension_semantics=("arbitrary",), vmem_limit_bytes=VMEM_LIMIT),
        name="gmlp_mixer",
    )(x2, w_in, ln_g, ln_b, w_s, b_s_t, w_out, mg, mb)


def _mlp_kernel(x_ref, w1_ref, w2_ref, g_ref, b_ref, o_ref, *, ff_chunk):
    tm = x_ref.shape[0]
    d_ff = w1_ref.shape[1]
    halves = [slice(0, tm // 2), slice(tm // 2, tm)]
    x16 = [x_ref[rows, :].astype(BF16) for rows in halves]
    acc = [None, None]
    for c in range(d_ff // ff_chunk):
        cols = slice(c * ff_chunk, (c + 1) * ff_chunk)
        hs = []
        for i in range(2):
            h = jnp.maximum(jnp.dot(x16[i], w1_ref[:, cols], preferred_element_type=F32), 0.0)
            hs.append((h * h).astype(BF16))
        for i in range(2):
            part = jnp.dot(hs[i], w2_ref[cols, :], preferred_element_type=F32)
            acc[i] = part if acc[i] is None else acc[i] + part
    for i, rows in enumerate(halves):
        o_ref[rows, :] = _layer_norm(ALPHA * x_ref[rows, :] + acc[i], g_ref[...], b_ref[...])


def _mlp_layer(x2, w1, w2, g, b, *, tm, ff_chunk):
    n, d = x2.shape
    d_ff = w1.shape[1]
    row = pl.BlockSpec((tm, d), lambda i: (i, 0))
    return pl.pallas_call(
        functools.partial(_mlp_kernel, ff_chunk=ff_chunk),
        grid=(n // tm,),
        in_specs=[row, _const_spec((d, d_ff)), _const_spec((d_ff, d)),
                  _const_spec((1, d)), _const_spec((1, d))],
        out_specs=row,
        out_shape=jax.ShapeDtypeStruct((n, d), F32),
        compiler_params=pltpu.CompilerParams(
            dimension_semantics=("arbitrary",), vmem_limit_bytes=VMEM_LIMIT),
        name="relu2_mlp",
    )(x2, w1, w2, g, b)


def _kv_kernel(x_ref, wkt_ref, wv_ref, kt_ref, v2_ref):
    x16 = x_ref[0].astype(BF16)
    kt = lax.dot_general(wkt_ref[...], x16, (((1,), (1,)), ((), ())),
                         preferred_element_type=F32)
    kt_ref[0] = kt.astype(BF16)
    v = jnp.dot(x16, wv_ref[...], preferred_element_type=F32).astype(BF16)
    kb = SB_KEY_BLOCK
    n_pairs, n_blocks = v2_ref.shape[1], v2_ref.shape[2]
    first_head = lax.broadcasted_iota(jnp.int32, (kb, LANES), 1) < LANES // 2
    zero = jnp.zeros((kb, LANES), BF16)
    for p in range(n_pairs):
        for r in range(n_blocks):
            blk = v[r * kb:(r + 1) * kb, p * LANES:(p + 1) * LANES]
            v2_ref[0, p, r, :kb, :] = jnp.where(first_head, blk, zero)
            v2_ref[0, p, r, kb:, :] = jnp.where(first_head, zero, blk)


def _kv_proj(x3, wk_t, wv, *, tm):
    b, s, d = x3.shape
    kb = SB_KEY_BLOCK
    return pl.pallas_call(
        _kv_kernel,
        grid=(b, s // tm),
        in_specs=[pl.BlockSpec((1, tm, d), lambda i, j: (i, j, 0)),
                  _const_spec((d, d)), _const_spec((d, d))],
        out_specs=[pl.BlockSpec((1, d, tm), lambda i, j: (i, 0, j)),
                   pl.BlockSpec((1, d // LANES, tm // kb, 2 * kb, LANES),
                                lambda i, j: (i, 0, j, 0, 0))],
        out_shape=[jax.ShapeDtypeStruct((b, d, s), BF16),
                   jax.ShapeDtypeStruct((b, d // LANES, s // kb, 2 * kb, LANES), BF16)],
        compiler_params=pltpu.CompilerParams(
            dimension_semantics=("arbitrary", "arbitrary"), vmem_limit_bytes=VMEM_LIMIT),
        name="kv_proj",
    )(x3, wk_t, wv)


def _sb_kernel(x_ref, wq_ref, kt_hbm, v2_hbm, wo_ref, g_ref, b_ref, o_ref,
               q_ref, acc_ref, rem_ref, kt_ref, v2_ref, kv_sem, *, tq):
    d = x_ref.shape[2]
    dh = d // SB_HEADS
    ts = SB_KEY_BLOCK
    nt = SB_TOP_ROWS
    assert 2 * dh == LANES and ts == LANES and tq % ts == 0
    scale = dh ** -0.5 * LOG2E
    bi, qi = pl.program_id(0), pl.program_id(1)

    def kv_copies(i):
        cols = pl.ds(pl.multiple_of(i * tq, tq), tq)
        blks = pl.ds(i * (tq // ts), tq // ts)
        return (pltpu.make_async_copy(kt_hbm.at[bi, :, cols], kt_ref.at[:, cols], kv_sem.at[0]),
                pltpu.make_async_copy(v2_hbm.at[bi, :, blks], v2_ref.at[:, blks], kv_sem.at[1]))

    @pl.when(qi == 0)
    def _():
        for c in kv_copies(0):
            c.start()

    for c in kv_copies(qi):
        c.wait()

    @pl.when(qi + 1 < pl.num_programs(1))
    def _():
        for c in kv_copies(qi + 1):
            c.start()

    x = x_ref[0]
    q = jnp.dot(x.astype(BF16), wq_ref[...], preferred_element_type=F32)
    q_ref[...] = (q * scale).astype(BF16)

    row = lax.broadcasted_iota(jnp.int32, (ts, ts), 0)
    col = lax.broadcasted_iota(jnp.int32, (ts, ts), 1)
    causal = col < row
    r2 = lax.broadcasted_iota(jnp.int32, (ts, 2 * ts), 0)
    c2 = lax.broadcasted_iota(jnp.int32, (ts, 2 * ts), 1)
    tri2 = jnp.logical_or(c2 >= ts, r2 > c2).astype(BF16)
    kt_zeros = jnp.zeros((dh, ts), BF16)
    heads = range(SB_HEADS)

    def logits(rows, h, kb):
        p, j = divmod(h, 2)
        q_p = q_ref[rows, p * LANES:(p + 1) * LANES]
        kt_h = kt_ref[h * dh:(h + 1) * dh, pl.ds(pl.multiple_of(kb * ts, ts), ts)]
        kt_h = jnp.concatenate([kt_h, kt_zeros] if j == 0 else [kt_zeros, kt_h], axis=0)
        return jnp.dot(q_p, kt_h, preferred_element_type=F32)

    def softplus2(z):
        return jnp.maximum(z, 0.0) + jnp.log2(1.0 + jnp.exp2(-jnp.abs(z)))

    def diag_step(r0, qb):
        all_rows, top_rows = pl.ds(r0, ts), pl.ds(r0, nt)
        kx = jnp.maximum(qb - 1, 0)
        has_x = qb >= 1
        z_d = [logits(all_rows, h, qb) for h in heads]
        z_x = [logits(top_rows, h, kx) for h in heads]
        lb_d, lb_x, splits = [], [], []
        for h in heads:
            sp_d = softplus2(z_d[h])
            sp_x = softplus2(z_x[h])
            lb_d.append(z_d[h] - sp_d)
            lb_x.append(z_x[h] - sp_x)
            sp_d = jnp.where(causal, sp_d, 0.0)
            sp_x = jnp.where(has_x, sp_x, 0.0)
            splits.append(jnp.concatenate([sp_d, sp_x], axis=0).astype(BF16))
        sums = [jnp.dot(splits[h], tri2, preferred_element_type=F32) for h in heads]
        w_d, w_x = [], []
        for h in heads:
            suffix_d, total_d = sums[h][:ts, :ts], sums[h][:ts, ts:]
            suffix_x, total_x = sums[h][ts:, :ts], sums[h][ts:, ts:]
            a_d = jnp.where(causal, jnp.exp2(lb_d[h] - suffix_d), 0.0)
            a_x = jnp.exp2(lb_x[h] - (suffix_x + total_d[:nt]))
            a_x = jnp.where(has_x, a_x, 0.0)
            rem_ref[h, :nt] = total_d[:nt] + total_x
            rem_ref[h, nt:] = total_d[nt:]
            w_d.append(a_d.astype(BF16))
            w_x.append(a_x.astype(BF16))
        for p in range(SB_HEADS // 2):
            lanes = slice(p * LANES, (p + 1) * LANES)
            av_d = jnp.dot(jnp.concatenate(w_d[2 * p:2 * p + 2], axis=1), v2_ref[p, qb],
                           preferred_element_type=F32)
            av_x = jnp.dot(jnp.concatenate(w_x[2 * p:2 * p + 2], axis=1), v2_ref[p, kx],
                           preferred_element_type=F32)
            acc_ref[top_rows, lanes] = av_d[:nt] + av_x
            acc_ref[pl.ds(r0 + nt, ts - nt), lanes] = av_d[nt:]

    def loop_step(r0, kb):
        top_rows, rest_rows = pl.ds(r0, nt), pl.ds(r0 + nt, ts - nt)
        kx = jnp.maximum(kb - 1, 0)
        has_x = kb >= 1
        zs = [jnp.concatenate([logits(top_rows, h, kx), logits(rest_rows, h, kb)], axis=0)
              for h in heads]
        lbs, splits = [], []
        for h in heads:
            sp = softplus2(zs[h])
            lbs.append(zs[h] - sp)
            sp = jnp.concatenate([jnp.where(has_x, sp[:nt], 0.0), sp[nt:]], axis=0)
            splits.append(sp.astype(BF16))
        sums = [jnp.dot(splits[h], tri2, preferred_element_type=F32) for h in heads]
        weights = []
        for h in heads:
            suffix, total = sums[h][:, :ts], sums[h][:, ts:]
            rem = rem_ref[h]
            a = jnp.exp2(lbs[h] - (suffix + rem))
            a = jnp.concatenate([jnp.where(has_x, a[:nt], 0.0), a[nt:]], axis=0)
            rem_ref[h] = rem + total
            weights.append(a.astype(BF16))
        for p in range(SB_HEADS // 2):
            lanes = slice(p * LANES, (p + 1) * LANES)
            a_p = jnp.concatenate(weights[2 * p:2 * p + 2], axis=1)
            acc_ref[top_rows, lanes] += jnp.dot(a_p[:nt], v2_ref[p, kx],
                                                preferred_element_type=F32)
            acc_ref[rest_rows, lanes] += jnp.dot(a_p[nt:], v2_ref[p, kb],
                                                 preferred_element_type=F32)

    def unfinished():
        m = rem_ref[0]
        for h in range(1, SB_HEADS):
            m = jnp.minimum(m, rem_ref[h])
        return (jnp.min(m) < F32_EXP2_ZERO).astype(jnp.int32)

    def sub_tile(sub, carry):
        r0 = pl.multiple_of(sub * ts, ts)
        qb = qi * (tq // ts) + sub
        diag_step(r0, qb)

        def body(c):
            kb, _ = c
            loop_step(r0, kb)
            return kb - 1, unfinished()

        def cond(c):
            kb, go = c
            return jnp.logical_and(kb >= 0, go > 0)

        lax.while_loop(cond, body, (qb - 1, jnp.int32(1)))
        return carry

    lax.fori_loop(0, tq // ts, sub_tile, 0)

    mix = jnp.dot(acc_ref[...].astype(BF16), wo_ref[...], preferred_element_type=F32)
    o_ref[0] = _layer_norm(ALPHA * x + mix, g_ref[...], b_ref[...])


def _sb_layer(x3, wq, kt, v2, wo, g, b, *, tq):
    bsz, s, d = x3.shape
    row = pl.BlockSpec((1, tq, d), lambda i, j: (i, j, 0))
    return pl.pallas_call(
        functools.partial(_sb_kernel, tq=tq),
        grid=(bsz, s // tq),
        in_specs=[row, _const_spec((d, d)),
                  pl.BlockSpec(memory_space=pl.ANY), pl.BlockSpec(memory_space=pl.ANY),
                  _const_spec((d, d)), _const_spec((1, d)), _const_spec((1, d))],
        out_specs=row,
        out_shape=jax.ShapeDtypeStruct((bsz, s, d), F32),
        scratch_shapes=[pltpu.VMEM((tq, d), BF16), pltpu.VMEM((tq, d), F32),
                        pltpu.VMEM((SB_HEADS, SB_KEY_BLOCK, LANES), F32),
                        pltpu.VMEM(kt.shape[1:], BF16), pltpu.VMEM(v2.shape[1:], BF16),
                        pltpu.SemaphoreType.DMA((2,))],
        compiler_params=pltpu.CompilerParams(
            dimension_semantics=("arbitrary", "arbitrary"), vmem_limit_bytes=VMEM_LIMIT),
        name="sb_mixer",
    )(x3, wq, kt, v2, wo, g, b)


def kernel(x, a_w_in, a_ln_g, a_ln_b, a_w_s, a_b_s, a_w_out, sb_w_k, sb_w_v, b_w_q, b_w_o,
           mix_ln_g, mix_ln_b, ffn_ln_g, ffn_ln_b, ffn_w1, ffn_w2):
    bsz, seq, d = x.shape
    n = bsz * seq
    vec = lambda p: p.reshape(1, -1).astype(F32)
    x2 = x.reshape(n, d)
    kt = v = None
    for l in range(DEPTH):
        if l < N_A:
            x2 = _gmlp_layer(
                x2, a_w_in[l].astype(BF16), vec(a_ln_g[l]), vec(a_ln_b[l]), a_w_s[l],
                jnp.transpose(a_b_s[l]), a_w_out[l].astype(BF16),
                vec(mix_ln_g[l]), vec(mix_ln_b[l]), tm=512)
        else:
            x3 = x2.reshape(bsz, seq, d)
            if l == N_A:
                kt, v = _kv_proj(x3, jnp.transpose(sb_w_k).astype(BF16),
                                 sb_w_v.astype(BF16), tm=512)
            j = l - N_A
            x2 = _sb_layer(x3, b_w_q[j].astype(BF16), kt, v, b_w_o[j].astype(BF16),
                           vec(mix_ln_g[l]), vec(mix_ln_b[l]), tq=256).reshape(n, d)
        x2 = _mlp_layer(x2, ffn_w1[l].astype(BF16), ffn_w2[l].astype(BF16),
                        vec(ffn_ln_g[l]), vec(ffn_ln_b[l]), tm=512, ff_chunk=1024)
    return x2.reshape(bsz, seq, d)
```

```python
import functools

import jax
import jax.numpy as jnp
from jax import lax
from jax.experimental import pallas as pl
from jax.experimental.pallas import tpu as pltpu

F32 = jnp.float32
BF16 = jnp.bfloat16

LANES = 128
CHUNK = 64
GMLP_BLOCK = 128
GMLP_GROUPS = 8
SB_HEADS = 16
LN_EPS = 1e-5
DEPTH = 4
N_A = DEPTH // 2
ALPHA = float((2 * DEPTH) ** 0.25)

LOG2E = 1.4426950408889634
F32_EXP2_ZERO = 127.0
SB_KEY_BLOCK = 128
SB_TOP_ROWS = 32

VMEM_LIMIT = 56 * 1024 * 1024


def _layer_norm(y, g, b):
    mu = jnp.mean(y, axis=-1, keepdims=True)
    d = y - mu
    var = jnp.mean(d * d, axis=-1, keepdims=True)
    return d * lax.rsqrt(var + LN_EPS) * g + b


def _gelu_tanh(x):
    c = 0.7978845608028654
    return 0.5 * x * (1.0 + jnp.tanh(c * (x + 0.044715 * (x * x * x))))


def _const_spec(shape):
    zeros = (0,) * len(shape)
    return pl.BlockSpec(shape, lambda *_: zeros, pipeline_mode=pl.Buffered(1))


def _gmlp_kernel(x_ref, win_ref, lng_ref, lnb_ref, ws_ref, bs_ref, wout_ref,
                 mg_ref, mb_ref, o_ref, gated_ref):
    tm, d = x_ref.shape
    w = wout_ref.shape[0]
    gd = w // GMLP_GROUPS
    x16 = x_ref[...].astype(BF16)
    v = _gelu_tanh(jnp.dot(x16, win_ref[:, w:], preferred_element_type=F32))
    u = _gelu_tanh(jnp.dot(x16, win_ref[:, :w], preferred_element_type=F32))
    v = _layer_norm(v, lng_ref[...], lnb_ref[...]).astype(BF16)
    t_idx = lax.broadcasted_iota(jnp.int32, (GMLP_BLOCK, GMLP_BLOCK), 0)
    s_idx = lax.broadcasted_iota(jnp.int32, (GMLP_BLOCK, GMLP_BLOCK), 1)
    allowed = (s_idx // CHUNK) <= (t_idx // CHUNK)
    for g in range(GMLP_GROUPS):
        ws = jnp.where(allowed, ws_ref[g], 0.0).astype(BF16)
        bias = bs_ref[:, g:g + 1]
        for r in range(tm // GMLP_BLOCK):
            rows = slice(r * GMLP_BLOCK, (r + 1) * GMLP_BLOCK)
            cols = slice(g * gd, (g + 1) * gd)
            s = jnp.dot(ws, v[rows, cols], preferred_element_type=F32) + bias
            gated_ref[rows, cols] = (u[rows, cols] * s).astype(BF16)
    half = tm // 2
    for r in range(2):
        rows = slice(r * half, (r + 1) * half)
        mix = jnp.dot(gated_ref[rows, :], wout_ref[...], preferred_element_type=F32)
        o_ref[rows, :] = _layer_norm(ALPHA * x_ref[rows, :] + mix, mg_ref[...], mb_ref[...])


def _gmlp_layer(x2, w_in, ln_g, ln_b, w_s, b_s_t, w_out, mg, mb, *, tm):
    n, d = x2.shape
    w = w_out.shape[0]
    row = pl.BlockSpec((tm, d), lambda i: (i, 0))
    return pl.pallas_call(
        _gmlp_kernel,
        grid=(n // tm,),
        in_specs=[row, _const_spec((d, 2 * w)), _const_spec((1, w)), _const_spec((1, w)),
                  _const_spec(w_s.shape), _const_spec(b_s_t.shape), _const_spec((w, d)),
                  _const_spec((1, d)), _const_spec((1, d))],
        out_specs=row,
        out_shape=jax.ShapeDtypeStruct((n, d), F32),
        scratch_shapes=[pltpu.VMEM((tm, w), BF16)],
        compiler_params=pltpu.CompilerParams(
            dimension_semantics=("arbitrary",), vmem_limit_bytes=VMEM_LIMIT),
        name="gmlp_mixer",
    )(x2, w_in, ln_g, ln_b, w_s, b_s_t, w_out, mg, mb)


def _mlp_kernel(x_ref, w1_ref, w2_ref, g_ref, b_ref, o_ref, *, ff_chunk):
    tm = x_ref.shape[0]
    d_ff = w1_ref.shape[1]
    halves = [slice(0, tm // 2), slice(tm // 2, tm)]
    x16 = [x_ref[rows, :].astype(BF16) for rows in halves]
    acc = [None, None]
    for c in range(d_ff // ff_chunk):
        cols = slice(c * ff_chunk, (c + 1) * ff_chunk)
        hs = []
        for i in range(2):
            h = jnp.maximum(jnp.dot(x16[i], w1_ref[:, cols], preferred_element_type=F32), 0.0)
            hs.append((h * h).astype(BF16))
        for i in range(2):
            part = jnp.dot(hs[i], w2_ref[cols, :], preferred_element_type=F32)
            acc[i] = part if acc[i] is None else acc[i] + part
    for i, rows in enumerate(halves):
        o_ref[rows, :] = _layer_norm(ALPHA * x_ref[rows, :] + acc[i], g_ref[...], b_ref[...])


def _mlp_layer(x2, w1, w2, g, b, *, tm, ff_chunk):
    n, d = x2.shape
    d_ff = w1.shape[1]
    row = pl.BlockSpec((tm, d), lambda i: (i, 0))
    return pl.pallas_call(
        functools.partial(_mlp_kernel, ff_chunk=ff_chunk),
        grid=(n // tm,),
        in_specs=[row, _const_spec((d, d_ff)), _const_spec((d_ff, d)),
                  _const_spec((1, d)), _const_spec((1, d))],
        out_specs=row,
        out_shape=jax.ShapeDtypeStruct((n, d), F32),
        compiler_params=pltpu.CompilerParams(
            dimension_semantics=("arbitrary",), vmem_limit_bytes=VMEM_LIMIT),
        name="relu2_mlp",
    )(x2, w1, w2, g, b)


def _kv_kernel(x_ref, wkt_ref, wv_ref, kt_ref, v2_ref):
    x16 = x_ref[0].astype(BF16)
    kt = lax.dot_general(wkt_ref[...], x16, (((1,), (1,)), ((), ())),
                         preferred_element_type=F32)
    kt_ref[0] = kt.astype(BF16)
    v = jnp.dot(x16, wv_ref[...], preferred_element_type=F32).astype(BF16)
    kb = SB_KEY_BLOCK
    n_pairs, n_blocks = v2_ref.shape[1], v2_ref.shape[2]
    first_head = lax.broadcasted_iota(jnp.int32, (kb, LANES), 1) < LANES // 2
    zero = jnp.zeros((kb, LANES), BF16)
    for p in range(n_pairs):
        for r in range(n_blocks):
            blk = v[r * kb:(r + 1) * kb, p * LANES:(p + 1) * LANES]
            v2_ref[0, p, r, :kb, :] = jnp.where(first_head, blk, zero)
            v2_ref[0, p, r, kb:, :] = jnp.where(first_head, zero, blk)


def _kv_proj(x3, wk_t, wv, *, tm):
    b, s, d = x3.shape
    kb = SB_KEY_BLOCK
    return pl.pallas_call(
        _kv_kernel,
        grid=(b, s // tm),
        in_specs=[pl.BlockSpec((1, tm, d), lambda i, j: (i, j, 0)),
                  _const_spec((d, d)), _const_spec((d, d))],
        out_specs=[pl.BlockSpec((1, d, tm), lambda i, j: (i, 0, j)),
                   pl.BlockSpec((1, d // LANES, tm // kb, 2 * kb, LANES),
                                lambda i, j: (i, 0, j, 0, 0))],
        out_shape=[jax.ShapeDtypeStruct((b, d, s), BF16),
                   jax.ShapeDtypeStruct((b, d // LANES, s // kb, 2 * kb, LANES), BF16)],
        compiler_params=pltpu.CompilerParams(
            dimension_semantics=("arbitrary", "arbitrary"), vmem_limit_bytes=VMEM_LIMIT),
        name="kv_proj",
    )(x3, wk_t, wv)


def _sb_kernel(x_ref, wq_ref, kt_hbm, v2_hbm, wo_ref, g_ref, b_ref, o_ref,
               q_ref, acc_ref, rem_ref, kt_ref, v2_ref, kv_sem, *, tq):
    d = x_ref.shape[2]
    dh = d // SB_HEADS
    ts = SB_KEY_BLOCK
    nt = SB_TOP_ROWS
    assert 2 * dh == LANES and ts == LANES and tq % ts == 0
    scale = dh ** -0.5 * LOG2E
    bi, qi = pl.program_id(0), pl.program_id(1)

    def kv_copies(i):
        cols = pl.ds(pl.multiple_of(i * tq, tq), tq)
        blks = pl.ds(i * (tq // ts), tq // ts)
        return (pltpu.make_async_copy(kt_hbm.at[bi, :, cols], kt_ref.at[:, cols], kv_sem.at[0]),
                pltpu.make_async_copy(v2_hbm.at[bi, :, blks], v2_ref.at[:, blks], kv_sem.at[1]))

    @pl.when(qi == 0)
    def _():
        for c in kv_copies(0):
            c.start()

    for c in kv_copies(qi):
        c.wait()

    @pl.when(qi + 1 < pl.num_programs(1))
    def _():
        for c in kv_copies(qi + 1):
            c.start()

    x = x_ref[0]
    q = jnp.dot(x.astype(BF16), wq_ref[...], preferred_element_type=F32)
    q_ref[...] = (q * scale).astype(BF16)

    row = lax.broadcasted_iota(jnp.int32, (ts, ts), 0)
    col = lax.broadcasted_iota(jnp.int32, (ts, ts), 1)
    causal = col < row
    r2 = lax.broadcasted_iota(jnp.int32, (ts, 2 * ts), 0)
    c2 = lax.broadcasted_iota(jnp.int32, (ts, 2 * ts), 1)
    tri2 = jnp.logical_or(c2 >= ts, r2 > c2).astype(BF16)
    kt_zeros = jnp.zeros((dh, ts), BF16)
    heads = range(SB_HEADS)

    def logits(rows, h, kb):
        p, j = divmod(h, 2)
        q_p = q_ref[rows, p * LANES:(p + 1) * LANES]
        kt_h = kt_ref[h * dh:(h + 1) * dh, pl.ds(pl.multiple_of(kb * ts, ts), ts)]
        kt_h = jnp.concatenate([kt_h, kt_zeros] if j == 0 else [kt_zeros, kt_h], axis=0)
        return jnp.dot(q_p, kt_h, preferred_element_type=F32)

    def softplus2(z):
        return jnp.maximum(z, 0.0) + jnp.log2(1.0 + jnp.exp2(-jnp.abs(z)))

    def diag_step(r0, qb):
        all_rows, top_rows = pl.ds(r0, ts), pl.ds(r0, nt)
        kx = jnp.maximum(qb - 1, 0)
        has_x = qb >= 1
        z_d = [logits(all_rows, h, qb) for h in heads]
        z_x = [logits(top_rows, h, kx) for h in heads]
        lb_d, lb_x, splits = [], [], []
        for h in heads:
            sp_d = softplus2(z_d[h])
            sp_x = softplus2(z_x[h])
            lb_d.append(z_d[h] - sp_d)
            lb_x.append(z_x[h] - sp_x)
            sp_d = jnp.where(causal, sp_d, 0.0)
            sp_x = jnp.where(has_x, sp_x, 0.0)
            splits.append(jnp.concatenate([sp_d, sp_x], axis=0).astype(BF16))
        sums = [jnp.dot(splits[h], tri2, preferred_element_type=F32) for h in heads]
        w_d, w_x = [], []
        for h in heads:
            suffix_d, total_d = sums[h][:ts, :ts], sums[h][:ts, ts:]
            suffix_x, total_x = sums[h][ts:, :ts], sums[h][ts:, ts:]
            a_d = jnp.where(causal, jnp.exp2(lb_d[h] - suffix_d), 0.0)
            a_x = jnp.exp2(lb_x[h] - (suffix_x + total_d[:nt]))
            a_x = jnp.where(has_x, a_x, 0.0)
            rem_ref[h, :nt] = total_d[:nt] + total_x
            rem_ref[h, nt:] = total_d[nt:]
            w_d.append(a_d.astype(BF16))
            w_x.append(a_x.astype(BF16))
        for p in range(SB_HEADS // 2):
            lanes = slice(p * LANES, (p + 1) * LANES)
            av_d = jnp.dot(jnp.concatenate(w_d[2 * p:2 * p + 2], axis=1), v2_ref[p, qb],
                           preferred_element_type=F32)
            av_x = jnp.dot(jnp.concatenate(w_x[2 * p:2 * p + 2], axis=1), v2_ref[p, kx],
                           preferred_element_type=F32)
            acc_ref[top_rows, lanes] = av_d[:nt] + av_x
            acc_ref[pl.ds(r0 + nt, ts - nt), lanes] = av_d[nt:]

    def loop_step(r0, kb):
        top_rows, rest_rows = pl.ds(r0, nt), pl.ds(r0 + nt, ts - nt)
        kx = jnp.maximum(kb - 1, 0)
        has_x = kb >= 1
        zs = [jnp.concatenate([logits(top_rows, h, kx), logits(rest_rows, h, kb)], axis=0)
              for h in heads]
        lbs, splits = [], []
        for h in heads:
            sp = softplus2(zs[h])
            lbs.append(zs[h] - sp)
            sp = jnp.concatenate([jnp.where(has_x, sp[:nt], 0.0), sp[nt:]], axis=0)
            splits.append(sp.astype(BF16))
        sums = [jnp.dot(splits[h], tri2, preferred_element_type=F32) for h in heads]
        weights = []
        for h in heads:
            suffix, total = sums[h][:, :ts], sums[h][:, ts:]
            rem = rem_ref[h]
            a = jnp.exp2(lbs[h] - (suffix + rem))
            a = jnp.concatenate([jnp.where(has_x, a[:nt], 0.0), a[nt:]], axis=0)
            rem_ref[h] = rem + total
            weights.append(a.astype(BF16))
        for p in range(SB_HEADS // 2):
            lanes = slice(p * LANES, (p + 1) * LANES)
            a_p = jnp.concatenate(weights[2 * p:2 * p + 2], axis=1)
            acc_ref[top_rows, lanes] += jnp.dot(a_p[:nt], v2_ref[p, kx],
                                                preferred_element_type=F32)
            acc_ref[rest_rows, lanes] += jnp.dot(a_p[nt:], v2_ref[p, kb],
                                                 preferred_element_type=F32)

    def unfinished():
        m = rem_ref[0]
        for h in range(1, SB_HEADS):
            m = jnp.minimum(m, rem_ref[h])
        return (jnp.min(m) < F32_EXP2_ZERO).astype(jnp.int32)

    def sub_tile(sub, carry):
        r0 = pl.multiple_of(sub * ts, ts)
        qb = qi * (tq // ts) + sub
        diag_step(r0, qb)

        def body(c):
            kb, _ = c
            loop_step(r0, kb)
            return kb - 1, unfinished()

        def cond(c):
            kb, go = c
            return jnp.logical_and(kb >= 0, go > 0)

        lax.while_loop(cond, body, (qb - 1, jnp.int32(1)))
        return carry

    lax.fori_loop(0, tq // ts, sub_tile, 0)

    mix = jnp.dot(acc_ref[...].astype(BF16), wo_ref[...], preferred_element_type=F32)
    o_ref[0] = _layer_norm(ALPHA * x + mix, g_ref[...], b_ref[...])


def _sb_layer(x3, wq, kt, v2, wo, g, b, *, tq):
    bsz, s, d = x3.shape
    row = pl.BlockSpec((1, tq, d), lambda i, j: (i, j, 0))
    return pl.pallas_call(
        functools.partial(_sb_kernel, tq=tq),
        grid=(bsz, s // tq),
        in_specs=[row, _const_spec((d, d)),
                  pl.BlockSpec(memory_space=pl.ANY), pl.BlockSpec(memory_space=pl.ANY),
                  _const_spec((d, d)), _const_spec((1, d)), _const_spec((1, d))],
        out_specs=row,
        out_shape=jax.ShapeDtypeStruct((bsz, s, d), F32),
        scratch_shapes=[pltpu.VMEM((tq, d), BF16), pltpu.VMEM((tq, d), F32),
                        pltpu.VMEM((SB_HEADS, SB_KEY_BLOCK, LANES), F32),
                        pltpu.VMEM(kt.shape[1:], BF16), pltpu.VMEM(v2.shape[1:], BF16),
                        pltpu.SemaphoreType.DMA((2,))],
        compiler_params=pltpu.CompilerParams(
            dimension_semantics=("arbitrary", "arbitrary"), vmem_limit_bytes=VMEM_LIMIT),
        name="sb_mixer",
    )(x3, wq, kt, v2, wo, g, b)


def kernel(x, a_w_in, a_ln_g, a_ln_b, a_w_s, a_b_s, a_w_out, sb_w_k, sb_w_v, b_w_q, b_w_o,
           mix_ln_g, mix_ln_b, ffn_ln_g, ffn_ln_b, ffn_w1, ffn_w2):
    bsz, seq, d = x.shape
    n = bsz * seq
    vec = lambda p: p.reshape(1, -1).astype(F32)
    x2 = x.reshape(n, d)
    kt = v = None
    for l in range(DEPTH):
        if l < N_A:
            x2 = _gmlp_layer(
                x2, a_w_in[l].astype(BF16), vec(a_ln_g[l]), vec(a_ln_b[l]), a_w_s[l],
                jnp.transpose(a_b_s[l]), a_w_out[l].astype(BF16),
                vec(mix_ln_g[l]), vec(mix_ln_b[l]), tm=1024)
        else:
            x3 = x2.reshape(bsz, seq, d)
            if l == N_A:
                kt, v = _kv_proj(x3, jnp.transpose(sb_w_k).astype(BF16),
                                 sb_w_v.astype(BF16), tm=1024)
            j = l - N_A
            x2 = _sb_layer(x3, b_w_q[j].astype(BF16), kt, v, b_w_o[j].astype(BF16),
                           vec(mix_ln_g[l]), vec(mix_ln_b[l]), tq=512).reshape(n, d)
        x2 = _mlp_layer(x2, ffn_w1[l].astype(BF16), ffn_w2[l].astype(BF16),
                        vec(ffn_ln_g[l]), vec(ffn_ln_b[l]), tm=1024, ff_chunk=1024)
    return x2.reshape(bsz, seq, d)
```

```python
import functools

import jax
import jax.numpy as jnp
from jax import lax
from jax.experimental import pallas as pl
from jax.experimental.pallas import tpu as pltpu

F32 = jnp.float32
BF16 = jnp.bfloat16

LANES = 128
CHUNK = 64
GMLP_BLOCK = 128
GMLP_GROUPS = 8
SB_HEADS = 16
LN_EPS = 1e-5
DEPTH = 4
N_A = DEPTH // 2
ALPHA = float((2 * DEPTH) ** 0.25)

LOG2E = 1.4426950408889634
F32_EXP2_ZERO = 127.0
SB_KEY_BLOCK = 128
SB_TOP_ROWS = 32

VMEM_LIMIT = 56 * 1024 * 1024


def _layer_norm(y, g, b):
    mu = jnp.mean(y, axis=-1, keepdims=True)
    d = y - mu
    var = jnp.mean(d * d, axis=-1, keepdims=True)
    return d * lax.rsqrt(var + LN_EPS) * g + b


def _gelu_tanh(x):
    c = 0.7978845608028654
    return 0.5 * x * (1.0 + jnp.tanh(x * (c + (c * 0.044715) * (x * x))))


def _const_spec(shape):
    zeros = (0,) * len(shape)
    return pl.BlockSpec(shape, lambda *_: zeros, pipeline_mode=pl.Buffered(1))


def _layer_spec(stack, layer):
    zeros = (0,) * (stack.ndim - 1)
    return pl.BlockSpec((None,) + stack.shape[1:], lambda *_: (layer,) + zeros,
                        pipeline_mode=pl.Buffered(1))


def _gmlp_kernel(x_ref, win_ref, lng_ref, lnb_ref, ws_ref, bs_ref, wout_ref,
                 mg_ref, mb_ref, o_ref, gated_ref):
    tm, d = x_ref.shape
    w = wout_ref.shape[0]
    gd = w // GMLP_GROUPS
    x16 = x_ref[...].astype(BF16)
    v = _gelu_tanh(jnp.dot(x16, win_ref[:, w:], preferred_element_type=F32))
    u = _gelu_tanh(jnp.dot(x16, win_ref[:, :w], preferred_element_type=F32))
    v = _layer_norm(v, lng_ref[...], lnb_ref[...]).astype(BF16)
    t_idx = lax.broadcasted_iota(jnp.int32, (GMLP_BLOCK, GMLP_BLOCK), 0)
    s_idx = lax.broadcasted_iota(jnp.int32, (GMLP_BLOCK, GMLP_BLOCK), 1)
    allowed = (s_idx // CHUNK) <= (t_idx // CHUNK)
    for g in range(GMLP_GROUPS):
        ws = jnp.where(allowed, ws_ref[g], 0.0).astype(BF16)
        bias = bs_ref[:, g:g + 1]
        for r in range(tm // GMLP_BLOCK):
            rows = slice(r * GMLP_BLOCK, (r + 1) * GMLP_BLOCK)
            cols = slice(g * gd, (g + 1) * gd)
            s = jnp.dot(ws, v[rows, cols], preferred_element_type=F32) + bias
            gated_ref[rows, cols] = (u[rows, cols] * s).astype(BF16)
    half = tm // 2
    for r in range(2):
        rows = slice(r * half, (r + 1) * half)
        mix = jnp.dot(gated_ref[rows, :], wout_ref[...], preferred_element_type=F32)
        o_ref[rows, :] = _layer_norm(ALPHA * x_ref[rows, :] + mix, mg_ref[...], mb_ref[...])


def _gmlp_layer(x2, layer, w_in, ln_g, ln_b, w_s, b_s_t, w_out, mg, mb, *, tm):
    n, d = x2.shape
    w = w_out.shape[1]
    row = pl.BlockSpec((tm, d), lambda i: (i, 0))
    return pl.pallas_call(
        _gmlp_kernel,
        grid=(n // tm,),
        in_specs=[row, _layer_spec(w_in, layer), _const_spec((1, w)), _const_spec((1, w)),
                  _layer_spec(w_s, layer), _const_spec(b_s_t.shape), _layer_spec(w_out, layer),
                  _const_spec((1, d)), _const_spec((1, d))],
        out_specs=row,
        out_shape=jax.ShapeDtypeStruct((n, d), F32),
        scratch_shapes=[pltpu.VMEM((tm, w), BF16)],
        compiler_params=pltpu.CompilerParams(
            dimension_semantics=("arbitrary",), vmem_limit_bytes=VMEM_LIMIT),
        name="gmlp_mixer",
    )(x2, w_in, ln_g, ln_b, w_s, b_s_t, w_out, mg, mb)


def _mlp_kernel(x_ref, w1_ref, w2_ref, g_ref, b_ref, o_ref, *, ff_chunk):
    tm = x_ref.shape[0]
    d_ff = w1_ref.shape[1]
    halves = [slice(0, tm // 2), slice(tm // 2, tm)]
    x16 = [x_ref[rows, :].astype(BF16) for rows in halves]
    acc = [None, None]
    for c in range(d_ff // ff_chunk):
        cols = slice(c * ff_chunk, (c + 1) * ff_chunk)
        hs = []
        for i in range(2):
            h = jnp.maximum(jnp.dot(x16[i], w1_ref[:, cols], preferred_element_type=F32), 0.0)
            hs.append((h * h).astype(BF16))
        for i in range(2):
            part = jnp.dot(hs[i], w2_ref[cols, :], preferred_element_type=F32)
            acc[i] = part if acc[i] is None else acc[i] + part
    for i, rows in enumerate(halves):
        o_ref[rows, :] = _layer_norm(ALPHA * x_ref[rows, :] + acc[i], g_ref[...], b_ref[...])


def _mlp_layer(x2, layer, w1, w2, g, b, *, tm, ff_chunk):
    n, d = x2.shape
    row = pl.BlockSpec((tm, d), lambda i: (i, 0))
    return pl.pallas_call(
        functools.partial(_mlp_kernel, ff_chunk=ff_chunk),
        grid=(n // tm,),
        in_specs=[row, _layer_spec(w1, layer), _layer_spec(w2, layer),
                  _const_spec((1, d)), _const_spec((1, d))],
        out_specs=row,
        out_shape=jax.ShapeDtypeStruct((n, d), F32),
        compiler_params=pltpu.CompilerParams(
            dimension_semantics=("arbitrary",), vmem_limit_bytes=VMEM_LIMIT),
        name="relu2_mlp",
    )(x2, w1, w2, g, b)


def _kv_kernel(x_ref, wkt_ref, wv_ref, kt_ref, v2_ref):
    x16 = x_ref[0].astype(BF16)
    kt = lax.dot_general(wkt_ref[...], x16, (((1,), (1,)), ((), ())),
                         preferred_element_type=F32)
    kt_ref[0] = kt.astype(BF16)
    v = jnp.dot(x16, wv_ref[...], preferred_element_type=F32).astype(BF16)
    kb = SB_KEY_BLOCK
    n_pairs, n_blocks = v2_ref.shape[1], v2_ref.shape[2]
    first_head = lax.broadcasted_iota(jnp.int32, (kb, LANES), 1) < LANES // 2
    zero = jnp.zeros((kb, LANES), BF16)
    for p in range(n_pairs):
        for r in range(n_blocks):
            blk = v[r * kb:(r + 1) * kb, p * LANES:(p + 1) * LANES]
            v2_ref[0, p, r, :kb, :] = jnp.where(first_head, blk, zero)
            v2_ref[0, p, r, kb:, :] = jnp.where(first_head, zero, blk)


def _kv_proj(x3, wk_t, wv, *, tm):
    b, s, d = x3.shape
    kb = SB_KEY_BLOCK
    return pl.pallas_call(
        _kv_kernel,
        grid=(b, s // tm),
        in_specs=[pl.BlockSpec((1, tm, d), lambda i, j: (i, j, 0)),
                  _const_spec((d, d)), _const_spec((d, d))],
        out_specs=[pl.BlockSpec((1, d, tm), lambda i, j: (i, 0, j)),
                   pl.BlockSpec((1, d // LANES, tm // kb, 2 * kb, LANES),
                                lambda i, j: (i, 0, j, 0, 0))],
        out_shape=[jax.ShapeDtypeStruct((b, d, s), BF16),
                   jax.ShapeDtypeStruct((b, d // LANES, s // kb, 2 * kb, LANES), BF16)],
        compiler_params=pltpu.CompilerParams(
            dimension_semantics=("arbitrary", "arbitrary"), vmem_limit_bytes=VMEM_LIMIT),
        name="kv_proj",
    )(x3, wk_t, wv)


def _sb_kernel(x_ref, wq_ref, kt_hbm, v2_hbm, wo_ref, g_ref, b_ref, o_ref,
               q_ref, acc_ref, rem_ref, kt_ref, v2_ref, kv_sem, *, tq):
    d = x_ref.shape[2]
    dh = d // SB_HEADS
    ts = SB_KEY_BLOCK
    nt = SB_TOP_ROWS
    assert 2 * dh == LANES and ts == LANES and tq % ts == 0
    scale = dh ** -0.5 * LOG2E
    bi, qi = pl.program_id(0), pl.program_id(1)

    def kv_copies(i):
        cols = pl.ds(pl.multiple_of(i * tq, tq), tq)
        blks = pl.ds(i * (tq // ts), tq // ts)
        return (pltpu.make_async_copy(kt_hbm.at[bi, :, cols], kt_ref.at[:, cols], kv_sem.at[0]),
                pltpu.make_async_copy(v2_hbm.at[bi, :, blks], v2_ref.at[:, blks], kv_sem.at[1]))

    @pl.when(qi == 0)
    def _():
        for c in kv_copies(0):
            c.start()

    for c in kv_copies(qi):
        c.wait()

    @pl.when(qi + 1 < pl.num_programs(1))
    def _():
        for c in kv_copies(qi + 1):
            c.start()

    q = jnp.dot(x_ref[0].astype(BF16), wq_ref[...], preferred_element_type=F32)
    q_ref[...] = (q * scale).astype(BF16)

    row = lax.broadcasted_iota(jnp.int32, (ts, ts), 0)
    col = lax.broadcasted_iota(jnp.int32, (ts, ts), 1)
    causal = col < row
    r2 = lax.broadcasted_iota(jnp.int32, (ts, 2 * ts), 0)
    c2 = lax.broadcasted_iota(jnp.int32, (ts, 2 * ts), 1)
    tri2 = jnp.logical_or(c2 >= ts, r2 > c2).astype(BF16)
    kt_zeros = jnp.zeros((dh, ts), BF16)
    heads = range(SB_HEADS)

    def logits(rows, h, kb):
        p, j = divmod(h, 2)
        q_p = q_ref[rows, p * LANES:(p + 1) * LANES]
        kt_h = kt_ref[h * dh:(h + 1) * dh, pl.ds(pl.multiple_of(kb * ts, ts), ts)]
        kt_h = jnp.concatenate([kt_h, kt_zeros] if j == 0 else [kt_zeros, kt_h], axis=0)
        return jnp.dot(q_p, kt_h, preferred_element_type=F32)

    def softplus2(z):
        return jnp.maximum(z, 0.0) + jnp.log2(1.0 + jnp.exp2(-jnp.abs(z)))

    def diag_step(r0, qb):
        all_rows, top_rows = pl.ds(r0, ts), pl.ds(r0, nt)
        kx = jnp.maximum(qb - 1, 0)
        has_x = qb >= 1
        z_d = [logits(all_rows, h, qb) for h in heads]
        z_x = [logits(top_rows, h, kx) for h in heads]
        lb_d, lb_x, sps = [], [], []
        for h in heads:
            sp_d = softplus2(z_d[h])
            sp_x = softplus2(z_x[h])
            lb_d.append(z_d[h] - sp_d)
            lb_x.append(z_x[h] - sp_x)
            sp_d = jnp.where(causal, sp_d, 0.0)
            sp_x = jnp.where(has_x, sp_x, 0.0)
            sps.append(jnp.concatenate([sp_d, sp_x], axis=0).astype(BF16))
        sums = [jnp.dot(sps[h], tri2, preferred_element_type=F32) for h in heads]
        w_d, w_x = [], []
        for h in heads:
            suffix_d, total_d = sums[h][:ts, :ts], sums[h][:ts, ts:]
            suffix_x, total_x = sums[h][ts:, :ts], sums[h][ts:, ts:]
            a_d = jnp.where(causal, jnp.exp2(lb_d[h] - suffix_d), 0.0)
            a_x = jnp.exp2(lb_x[h] - (suffix_x + total_d[:nt]))
            a_x = jnp.where(has_x, a_x, 0.0)
            rem_ref[h, :nt] = total_d[:nt] + total_x
            rem_ref[h, nt:] = total_d[nt:]
            w_d.append(a_d.astype(BF16))
            w_x.append(a_x.astype(BF16))
        for p in range(SB_HEADS // 2):
            lanes = slice(p * LANES, (p + 1) * LANES)
            av_d = jnp.dot(jnp.concatenate(w_d[2 * p:2 * p + 2], axis=1), v2_ref[p, qb],
                           preferred_element_type=F32)
            av_x = jnp.dot(jnp.concatenate(w_x[2 * p:2 * p + 2], axis=1), v2_ref[p, kx],
                           preferred_element_type=F32)
            acc_ref[top_rows, lanes] = av_d[:nt] + av_x
            acc_ref[pl.ds(r0 + nt, ts - nt), lanes] = av_d[nt:]

    def loop_step(r0, kb):
        top_rows, rest_rows = pl.ds(r0, nt), pl.ds(r0 + nt, ts - nt)
        kx = jnp.maximum(kb - 1, 0)
        has_x = kb >= 1
        zs = [jnp.concatenate([logits(top_rows, h, kx), logits(rest_rows, h, kb)], axis=0)
              for h in heads]
        lbs, sps = [], []
        for h in heads:
            sp = softplus2(zs[h])
            lbs.append(zs[h] - sp)
            sp = jnp.concatenate([jnp.where(has_x, sp[:nt], 0.0), sp[nt:]], axis=0)
            sps.append(sp.astype(BF16))
        sums = [jnp.dot(sps[h], tri2, preferred_element_type=F32) for h in heads]
        weights = []
        for h in heads:
            suffix, total = sums[h][:, :ts], sums[h][:, ts:]
            rem = rem_ref[h]
            a = jnp.exp2(lbs[h] - (suffix + rem))
            a = jnp.concatenate([jnp.where(has_x, a[:nt], 0.0), a[nt:]], axis=0)
            rem_ref[h] = rem + total
            weights.append(a.astype(BF16))
        for p in range(SB_HEADS // 2):
            lanes = slice(p * LANES, (p + 1) * LANES)
            a_p = jnp.concatenate(weights[2 * p:2 * p + 2], axis=1)
            acc_ref[top_rows, lanes] += jnp.dot(a_p[:nt], v2_ref[p, kx],
                                                preferred_element_type=F32)
            acc_ref[rest_rows, lanes] += jnp.dot(a_p[nt:], v2_ref[p, kb],
                                                 preferred_element_type=F32)

    def unfinished():
        m = rem_ref[0]
        for h in range(1, SB_HEADS):
            m = jnp.minimum(m, rem_ref[h])
        return (jnp.min(m) < F32_EXP2_ZERO).astype(jnp.int32)

    def sub_tile(sub, carry):
        r0 = pl.multiple_of(sub * ts, ts)
        qb = qi * (tq // ts) + sub
        diag_step(r0, qb)

        def body(c):
            kb, _ = c
            loop_step(r0, kb)
            return kb - 1, unfinished()

        def cond(c):
            kb, go = c
            return jnp.logical_and(kb >= 0, go > 0)

        lax.while_loop(cond, body, (qb - 1, jnp.int32(1)))
        return carry

    lax.fori_loop(0, tq // ts, sub_tile, 0)

    for r in range(2):
        rows = slice(r * (tq // 2), (r + 1) * (tq // 2))
        mix = jnp.dot(acc_ref[rows, :].astype(BF16), wo_ref[...], preferred_element_type=F32)
        o_ref[0, rows, :] = _layer_norm(ALPHA * x_ref[0, rows, :] + mix, g_ref[...], b_ref[...])


def _sb_layer(x3, layer, wq, kt, v2, wo, g, b, *, tq):
    bsz, s, d = x3.shape
    row = pl.BlockSpec((1, tq, d), lambda i, j: (i, j, 0))
    return pl.pallas_call(
        functools.partial(_sb_kernel, tq=tq),
        grid=(bsz, s // tq),
        in_specs=[row, _layer_spec(wq, layer),
                  pl.BlockSpec(memory_space=pl.ANY), pl.BlockSpec(memory_space=pl.ANY),
                  _layer_spec(wo, layer), _const_spec((1, d)), _const_spec((1, d))],
        out_specs=row,
        out_shape=jax.ShapeDtypeStruct((bsz, s, d), F32),
        scratch_shapes=[pltpu.VMEM((tq, d), BF16), pltpu.VMEM((tq, d), F32),
                        pltpu.VMEM((SB_HEADS, SB_KEY_BLOCK, LANES), F32),
                        pltpu.VMEM(kt.shape[1:], BF16), pltpu.VMEM(v2.shape[1:], BF16),
                        pltpu.SemaphoreType.DMA((2,))],
        compiler_params=pltpu.CompilerParams(
            dimension_semantics=("arbitrary", "arbitrary"), vmem_limit_bytes=VMEM_LIMIT),
        name="sb_mixer",
    )(x3, wq, kt, v2, wo, g, b)


def kernel(x, a_w_in, a_ln_g, a_ln_b, a_w_s, a_b_s, a_w_out, sb_w_k, sb_w_v, b_w_q, b_w_o,
           mix_ln_g, mix_ln_b, ffn_ln_g, ffn_ln_b, ffn_w1, ffn_w2):
    bsz, seq, d = x.shape
    n = bsz * seq
    vec = lambda p: p.reshape(1, -1).astype(F32)
    a_w_in, a_w_out, b_w_q, b_w_o, ffn_w1, ffn_w2 = (
        w.astype(BF16) for w in (a_w_in, a_w_out, b_w_q, b_w_o, ffn_w1, ffn_w2))
    x2 = x.reshape(n, d)
    kt = v2 = None
    for l in range(DEPTH):
        if l < N_A:
            x2 = _gmlp_layer(
                x2, l, a_w_in, vec(a_ln_g[l]), vec(a_ln_b[l]), a_w_s, jnp.transpose(a_b_s[l]),
                a_w_out, vec(mix_ln_g[l]), vec(mix_ln_b[l]), tm=1024)
        else:
            x3 = x2.reshape(bsz, seq, d)
            if l == N_A:
                kt, v2 = _kv_proj(x3, jnp.transpose(sb_w_k).astype(BF16),
                                  sb_w_v.astype(BF16), tm=1024)
            x2 = _sb_layer(x3, l - N_A, b_w_q, kt, v2, b_w_o,
                           vec(mix_ln_g[l]), vec(mix_ln_b[l]), tq=512).reshape(n, d)
        x2 = _mlp_layer(x2, l, ffn_w1, ffn_w2, vec(ffn_ln_g[l]), vec(ffn_ln_b[l]),
                        tm=1024, ff_chunk=1024)
    return x2.reshape(bsz, seq, d)
```

```python
import functools

import jax
import jax.numpy as jnp
from jax import lax
from jax.experimental import pallas as pl
from jax.experimental.pallas import tpu as pltpu

F32 = jnp.float32
BF16 = jnp.bfloat16

LANES = 128
CHUNK = 64
GMLP_BLOCK = 128
GMLP_GROUPS = 8
SB_HEADS = 16
LN_EPS = 1e-5
DEPTH = 4
N_A = DEPTH // 2
ALPHA = float((2 * DEPTH) ** 0.25)

LOG2E = 1.4426950408889634
F32_EXP2_ZERO = 127.0
SB_KEY_BLOCK = 128
SB_TOP_ROWS = 32

VMEM_LIMIT = 56 * 1024 * 1024


def _layer_norm(y, g, b):
    mu = jnp.mean(y, axis=-1, keepdims=True)
    d = y - mu
    var = jnp.mean(d * d, axis=-1, keepdims=True)
    return d * lax.rsqrt(var + LN_EPS) * g + b


def _gelu_tanh(x):
    c = 0.7978845608028654
    return 0.5 * x * (1.0 + jnp.tanh(x * (c + (c * 0.044715) * (x * x))))


def _const_spec(shape):
    zeros = (0,) * len(shape)
    return pl.BlockSpec(shape, lambda *_: zeros, pipeline_mode=pl.Buffered(1))


def _layer_spec(stack, layer):
    zeros = (0,) * (stack.ndim - 1)
    return pl.BlockSpec((None,) + stack.shape[1:], lambda *_: (layer,) + zeros,
                        pipeline_mode=pl.Buffered(1))


def _gmlp_kernel(x_ref, win_ref, lng_ref, lnb_ref, ws_ref, bs_ref, wout_ref,
                 mg_ref, mb_ref, o_ref, gated_ref):
    tm, d = x_ref.shape
    w = wout_ref.shape[0]
    gd = w // GMLP_GROUPS
    x16 = x_ref[...].astype(BF16)
    v = _gelu_tanh(jnp.dot(x16, win_ref[:, w:], preferred_element_type=F32))
    u = _gelu_tanh(jnp.dot(x16, win_ref[:, :w], preferred_element_type=F32))
    v = _layer_norm(v, lng_ref[...], lnb_ref[...]).astype(BF16)
    t_idx = lax.broadcasted_iota(jnp.int32, (GMLP_BLOCK, GMLP_BLOCK), 0)
    s_idx = lax.broadcasted_iota(jnp.int32, (GMLP_BLOCK, GMLP_BLOCK), 1)
    allowed = (s_idx // CHUNK) <= (t_idx // CHUNK)
    for g in range(GMLP_GROUPS):
        ws = jnp.where(allowed, ws_ref[g], 0.0).astype(BF16)
        bias = bs_ref[:, g:g + 1]
        for r in range(tm // GMLP_BLOCK):
            rows = slice(r * GMLP_BLOCK, (r + 1) * GMLP_BLOCK)
            cols = slice(g * gd, (g + 1) * gd)
            s = jnp.dot(ws, v[rows, cols], preferred_element_type=F32) + bias
            gated_ref[rows, cols] = (u[rows, cols] * s).astype(BF16)
    half = tm // 2
    for r in range(2):
        rows = slice(r * half, (r + 1) * half)
        mix = jnp.dot(gated_ref[rows, :], wout_ref[...], preferred_element_type=F32)
        o_ref[rows, :] = _layer_norm(ALPHA * x_ref[rows, :] + mix, mg_ref[...], mb_ref[...])


def _gmlp_layer(x2, layer, w_in, ln_g, ln_b, w_s, b_s_t, w_out, mg, mb, *, tm):
    n, d = x2.shape
    w = w_out.shape[1]
    row = pl.BlockSpec((tm, d), lambda i: (i, 0))
    return pl.pallas_call(
        _gmlp_kernel,
        grid=(n // tm,),
        in_specs=[row, _layer_spec(w_in, layer), _const_spec((1, w)), _const_spec((1, w)),
                  _layer_spec(w_s, layer), _const_spec(b_s_t.shape), _layer_spec(w_out, layer),
                  _const_spec((1, d)), _const_spec((1, d))],
        out_specs=row,
        out_shape=jax.ShapeDtypeStruct((n, d), F32),
        scratch_shapes=[pltpu.VMEM((tm, w), BF16)],
        compiler_params=pltpu.CompilerParams(
            dimension_semantics=("arbitrary",), vmem_limit_bytes=VMEM_LIMIT),
        name="gmlp_mixer",
    )(x2, w_in, ln_g, ln_b, w_s, b_s_t, w_out, mg, mb)


def _mlp_kernel(x_ref, w1_ref, w2_ref, g_ref, b_ref, o_ref, *, ff_chunk):
    tm = x_ref.shape[0]
    d_ff = w1_ref.shape[1]
    halves = [slice(0, tm // 2), slice(tm // 2, tm)]
    x16 = [x_ref[rows, :].astype(BF16) for rows in halves]
    acc = [None, None]
    for c in range(d_ff // ff_chunk):
        cols = slice(c * ff_chunk, (c + 1) * ff_chunk)
        hs = []
        for i in range(2):
            h = jnp.maximum(jnp.dot(x16[i], w1_ref[:, cols], preferred_element_type=F32), 0.0)
            hs.append((h * h).astype(BF16))
        for i in range(2):
            part = jnp.dot(hs[i], w2_ref[cols, :], preferred_element_type=F32)
            acc[i] = part if acc[i] is None else acc[i] + part
    for i, rows in enumerate(halves):
        o_ref[rows, :] = _layer_norm(ALPHA * x_ref[rows, :] + acc[i], g_ref[...], b_ref[...])


def _mlp_layer(x2, layer, w1, w2, g, b, *, tm, ff_chunk):
    n, d = x2.shape
    row = pl.BlockSpec((tm, d), lambda i: (i, 0))
    return pl.pallas_call(
        functools.partial(_mlp_kernel, ff_chunk=ff_chunk),
        grid=(n // tm,),
        in_specs=[row, _layer_spec(w1, layer), _layer_spec(w2, layer),
                  _const_spec((1, d)), _const_spec((1, d))],
        out_specs=row,
        out_shape=jax.ShapeDtypeStruct((n, d), F32),
        compiler_params=pltpu.CompilerParams(
            dimension_semantics=("arbitrary",), vmem_limit_bytes=VMEM_LIMIT),
        name="relu2_mlp",
    )(x2, w1, w2, g, b)


def _kv_kernel(x_ref, wkt_ref, wv_ref, kt_ref, v2_ref):
    x16 = x_ref[0].astype(BF16)
    kt = lax.dot_general(wkt_ref[...], x16, (((1,), (1,)), ((), ())),
                         preferred_element_type=F32)
    kt_ref[0] = kt.astype(BF16)
    v = jnp.dot(x16, wv_ref[...], preferred_element_type=F32).astype(BF16)
    kb = SB_KEY_BLOCK
    n_pairs, n_blocks = v2_ref.shape[1], v2_ref.shape[2]
    first_head = lax.broadcasted_iota(jnp.int32, (kb, LANES), 1) < LANES // 2
    zero = jnp.zeros((kb, LANES), BF16)
    for p in range(n_pairs):
        for r in range(n_blocks):
            blk = v[r * kb:(r + 1) * kb, p * LANES:(p + 1) * LANES]
            v2_ref[0, p, r, :kb, :] = jnp.where(first_head, blk, zero)
            v2_ref[0, p, r, kb:, :] = jnp.where(first_head, zero, blk)


def _kv_proj(x3, wk_t, wv, *, tm):
    b, s, d = x3.shape
    kb = SB_KEY_BLOCK
    return pl.pallas_call(
        _kv_kernel,
        grid=(b, s // tm),
        in_specs=[pl.BlockSpec((1, tm, d), lambda i, j: (i, j, 0)),
                  _const_spec((d, d)), _const_spec((d, d))],
        out_specs=[pl.BlockSpec((1, d, tm), lambda i, j: (i, 0, j)),
                   pl.BlockSpec((1, d // LANES, tm // kb, 2 * kb, LANES),
                                lambda i, j: (i, 0, j, 0, 0))],
        out_shape=[jax.ShapeDtypeStruct((b, d, s), BF16),
                   jax.ShapeDtypeStruct((b, d // LANES, s // kb, 2 * kb, LANES), BF16)],
        compiler_params=pltpu.CompilerParams(
            dimension_semantics=("arbitrary", "arbitrary"), vmem_limit_bytes=VMEM_LIMIT),
        name="kv_proj",
    )(x3, wk_t, wv)


def _sb_kernel(x_ref, wq_ref, kt_hbm, v2_hbm, wo_ref, g_ref, b_ref, o_ref,
               q_ref, acc_ref, rem_ref, kt_ref, v2_ref, kv_sem, *, tq):
    d = x_ref.shape[2]
    dh = d // SB_HEADS
    ts = SB_KEY_BLOCK
    nt = SB_TOP_ROWS
    assert 2 * dh == LANES and ts == LANES and tq % ts == 0
    scale = dh ** -0.5 * LOG2E
    bi, qi = pl.program_id(0), pl.program_id(1)

    def kv_copies(i):
        cols = pl.ds(pl.multiple_of(i * tq, tq), tq)
        blks = pl.ds(i * (tq // ts), tq // ts)
        return (pltpu.make_async_copy(kt_hbm.at[bi, :, cols], kt_ref.at[:, cols], kv_sem.at[0]),
                pltpu.make_async_copy(v2_hbm.at[bi, :, blks], v2_ref.at[:, blks], kv_sem.at[1]))

    @pl.when(qi == 0)
    def _():
        for c in kv_copies(0):
            c.start()

    for c in kv_copies(qi):
        c.wait()

    @pl.when(qi + 1 < pl.num_programs(1))
    def _():
        for c in kv_copies(qi + 1):
            c.start()

    q = jnp.dot(x_ref[0].astype(BF16), wq_ref[...], preferred_element_type=F32)
    q_ref[...] = (q * scale).astype(BF16)

    row = lax.broadcasted_iota(jnp.int32, (ts, ts), 0)
    col = lax.broadcasted_iota(jnp.int32, (ts, ts), 1)
    causal = col < row
    r2 = lax.broadcasted_iota(jnp.int32, (ts, 2 * ts), 0)
    c2 = lax.broadcasted_iota(jnp.int32, (ts, 2 * ts), 1)
    tri2 = jnp.logical_or(c2 >= ts, r2 > c2).astype(BF16)
    kt_zeros = jnp.zeros((dh, ts), BF16)
    heads = range(SB_HEADS)

    def logits(rows, h, kb):
        p, j = divmod(h, 2)
        q_p = q_ref[rows, p * LANES:(p + 1) * LANES]
        kt_h = kt_ref[h * dh:(h + 1) * dh, pl.ds(pl.multiple_of(kb * ts, ts), ts)]
        kt_h = jnp.concatenate([kt_h, kt_zeros] if j == 0 else [kt_zeros, kt_h], axis=0)
        return jnp.dot(q_p, kt_h, preferred_element_type=F32)

    def softplus2(z):
        return jnp.maximum(z, 0.0) + jnp.log2(1.0 + jnp.exp2(-jnp.abs(z)))

    def diag_step(r0, qb):
        all_rows, top_rows = pl.ds(r0, ts), pl.ds(r0, nt)
        kx = jnp.maximum(qb - 1, 0)
        has_x = qb >= 1
        z_d = [logits(all_rows, h, qb) for h in heads]
        z_x = [logits(top_rows, h, kx) for h in heads]
        lb_d, lb_x, sps = [], [], []
        for h in heads:
            sp_d = softplus2(z_d[h])
            sp_x = softplus2(z_x[h])
            lb_d.append(z_d[h] - sp_d)
            lb_x.append(z_x[h] - sp_x)
            sp_d = jnp.where(causal, sp_d, 0.0)
            sp_x = jnp.where(has_x, sp_x, 0.0)
            sps.append(jnp.concatenate([sp_d, sp_x], axis=0).astype(BF16))
        sums = [jnp.dot(sps[h], tri2, preferred_element_type=F32) for h in heads]
        w_d, w_x = [], []
        for h in heads:
            suffix_d, total_d = sums[h][:ts, :ts], sums[h][:ts, ts:]
            suffix_x, total_x = sums[h][ts:, :ts], sums[h][ts:, ts:]
            a_d = jnp.where(causal, jnp.exp2(lb_d[h] - suffix_d), 0.0)
            a_x = jnp.exp2(lb_x[h] - (suffix_x + total_d[:nt]))
            a_x = jnp.where(has_x, a_x, 0.0)
            rem_ref[h, :nt] = total_d[:nt] + total_x
            rem_ref[h, nt:] = total_d[nt:]
            w_d.append(a_d.astype(BF16))
            w_x.append(a_x.astype(BF16))
        for p in range(SB_HEADS // 2):
            lanes = slice(p * LANES, (p + 1) * LANES)
            av_d = jnp.dot(jnp.concatenate(w_d[2 * p:2 * p + 2], axis=1), v2_ref[p, qb],
                           preferred_element_type=F32)
            av_x = jnp.dot(jnp.concatenate(w_x[2 * p:2 * p + 2], axis=1), v2_ref[p, kx],
                           preferred_element_type=F32)
            acc_ref[top_rows, lanes] = av_d[:nt] + av_x
            acc_ref[pl.ds(r0 + nt, ts - nt), lanes] = av_d[nt:]

    def loop_step(r0, kb):
        top_rows, rest_rows = pl.ds(r0, nt), pl.ds(r0 + nt, ts - nt)
        kx = jnp.maximum(kb - 1, 0)
        has_x = kb >= 1
        zs = [jnp.concatenate([logits(top_rows, h, kx), logits(rest_rows, h, kb)], axis=0)
              for h in heads]
        lbs, sps = [], []
        for h in heads:
            sp = softplus2(zs[h])
            lbs.append(zs[h] - sp)
            sp = jnp.concatenate([jnp.where(has_x, sp[:nt], 0.0), sp[nt:]], axis=0)
            sps.append(sp.astype(BF16))
        sums = [jnp.dot(sps[h], tri2, preferred_element_type=F32) for h in heads]
        weights = []
        for h in heads:
            suffix, total = sums[h][:, :ts], sums[h][:, ts:]
            rem = rem_ref[h]
            a = jnp.exp2(lbs[h] - (suffix + rem))
            a = jnp.concatenate([jnp.where(has_x, a[:nt], 0.0), a[nt:]], axis=0)
            rem_ref[h] = rem + total
            weights.append(a.astype(BF16))
        for p in range(SB_HEADS // 2):
            lanes = slice(p * LANES, (p + 1) * LANES)
            a_p = jnp.concatenate(weights[2 * p:2 * p + 2], axis=1)
            acc_ref[top_rows, lanes] += jnp.dot(a_p[:nt], v2_ref[p, kx],
                                                preferred_element_type=F32)
            acc_ref[rest_rows, lanes] += jnp.dot(a_p[nt:], v2_ref[p, kb],
                                                 preferred_element_type=F32)

    def fused_step(r0, qb):
        all_rows, top_rows = pl.ds(r0, ts), pl.ds(r0, nt)
        rest_rows = pl.ds(r0 + nt, ts - nt)
        k1 = qb - 1
        k2 = jnp.maximum(qb - 2, 0)
        has_2 = qb >= 2
        z_d = [logits(all_rows, h, qb) for h in heads]
        z_x = [logits(top_rows, h, k1) for h in heads]
        z_l = [jnp.concatenate([logits(top_rows, h, k2), logits(rest_rows, h, k1)], axis=0)
               for h in heads]
        lb_d, lb_x, lb_l, sps = [], [], [], []
        for h in heads:
            sp_d, sp_x, sp_l = softplus2(z_d[h]), softplus2(z_x[h]), softplus2(z_l[h])
            lb_d.append(z_d[h] - sp_d)
            lb_x.append(z_x[h] - sp_x)
            lb_l.append(z_l[h] - sp_l)
            sp_d = jnp.where(causal, sp_d, 0.0)
            sp_l = jnp.concatenate([jnp.where(has_2, sp_l[:nt], 0.0), sp_l[nt:]], axis=0)
            sps.append(jnp.concatenate([sp_d, sp_x, sp_l], axis=0).astype(BF16))
        sums = [jnp.dot(sps[h], tri2, preferred_element_type=F32) for h in heads]
        w_d, w_x, w_l = [], [], []
        for h in heads:
            suffix_d, total_d = sums[h][:ts, :ts], sums[h][:ts, ts:]
            suffix_x, total_x = sums[h][ts:ts + nt, :ts], sums[h][ts:ts + nt, ts:]
            suffix_l, total_l = sums[h][ts + nt:, :ts], sums[h][ts + nt:, ts:]
            rem = jnp.concatenate([total_d[:nt] + total_x, total_d[nt:]], axis=0)
            a_d = jnp.where(causal, jnp.exp2(lb_d[h] - suffix_d), 0.0)
            a_x = jnp.exp2(lb_x[h] - (suffix_x + total_d[:nt]))
            a_l = jnp.exp2(lb_l[h] - (suffix_l + rem))
            a_l = jnp.concatenate([jnp.where(has_2, a_l[:nt], 0.0), a_l[nt:]], axis=0)
            rem_ref[h] = rem + total_l
            w_d.append(a_d.astype(BF16))
            w_x.append(a_x.astype(BF16))
            w_l.append(a_l.astype(BF16))
        for p in range(SB_HEADS // 2):
            lanes = slice(p * LANES, (p + 1) * LANES)
            pair = lambda ws: jnp.concatenate(ws[2 * p:2 * p + 2], axis=1)
            a_l = pair(w_l)
            av_d = jnp.dot(pair(w_d), v2_ref[p, qb], preferred_element_type=F32)
            av_x = jnp.dot(pair(w_x), v2_ref[p, k1], preferred_element_type=F32)
            av_lt = jnp.dot(a_l[:nt], v2_ref[p, k2], preferred_element_type=F32)
            av_lr = jnp.dot(a_l[nt:], v2_ref[p, k1], preferred_element_type=F32)
            acc_ref[top_rows, lanes] = av_d[:nt] + av_x + av_lt
            acc_ref[rest_rows, lanes] = av_d[nt:] + av_lr

    def unfinished():
        m = rem_ref[0]
        for h in range(1, SB_HEADS):
            m = jnp.minimum(m, rem_ref[h])
        return (jnp.min(m) < F32_EXP2_ZERO).astype(jnp.int32)

    def sub_tile(sub, carry):
        r0 = pl.multiple_of(sub * ts, ts)
        qb = qi * (tq // ts) + sub

        def first_block():
            diag_step(r0, qb)
            return jnp.int32(0)

        def later_block():
            fused_step(r0, qb)
            return unfinished()

        go = lax.cond(qb == 0, first_block, later_block)

        def body(c):
            kb, _ = c
            loop_step(r0, kb)
            return kb - 1, unfinished()

        def cond(c):
            kb, go = c
            return jnp.logical_and(kb >= 0, go > 0)

        lax.while_loop(cond, body, (qb - 2, go))
        return carry

    lax.fori_loop(0, tq // ts, sub_tile, 0)

    for r in range(2):
        rows = slice(r * (tq // 2), (r + 1) * (tq // 2))
        mix = jnp.dot(acc_ref[rows, :].astype(BF16), wo_ref[...], preferred_element_type=F32)
        o_ref[0, rows, :] = _layer_norm(ALPHA * x_ref[0, rows, :] + mix, g_ref[...], b_ref[...])


def _sb_layer(x3, layer, wq, kt, v2, wo, g, b, *, tq):
    bsz, s, d = x3.shape
    row = pl.BlockSpec((1, tq, d), lambda i, j: (i, j, 0))
    return pl.pallas_call(
        functools.partial(_sb_kernel, tq=tq),
        grid=(bsz, s // tq),
        in_specs=[row, _layer_spec(wq, layer),
                  pl.BlockSpec(memory_space=pl.ANY), pl.BlockSpec(memory_space=pl.ANY),
                  _layer_spec(wo, layer), _const_spec((1, d)), _const_spec((1, d))],
        out_specs=row,
        out_shape=jax.ShapeDtypeStruct((bsz, s, d), F32),
        scratch_shapes=[pltpu.VMEM((tq, d), BF16), pltpu.VMEM((tq, d), F32),
                        pltpu.VMEM((SB_HEADS, SB_KEY_BLOCK, LANES), F32),
                        pltpu.VMEM(kt.shape[1:], BF16), pltpu.VMEM(v2.shape[1:], BF16),
                        pltpu.SemaphoreType.DMA((2,))],
        compiler_params=pltpu.CompilerParams(
            dimension_semantics=("arbitrary", "arbitrary"), vmem_limit_bytes=VMEM_LIMIT),
        name="sb_mixer",
    )(x3, wq, kt, v2, wo, g, b)


def kernel(x, a_w_in, a_ln_g, a_ln_b, a_w_s, a_b_s, a_w_out, sb_w_k, sb_w_v, b_w_q, b_w_o,
           mix_ln_g, mix_ln_b, ffn_ln_g, ffn_ln_b, ffn_w1, ffn_w2):
    bsz, seq, d = x.shape
    n = bsz * seq
    vec = lambda p: p.reshape(1, -1).astype(F32)
    a_w_in, a_w_out, b_w_q, b_w_o, ffn_w1, ffn_w2 = (
        w.astype(BF16) for w in (a_w_in, a_w_out, b_w_q, b_w_o, ffn_w1, ffn_w2))
    x2 = x.reshape(n, d)
    kt = v2 = None
    for l in range(DEPTH):
        if l < N_A:
            x2 = _gmlp_layer(
                x2, l, a_w_in, vec(a_ln_g[l]), vec(a_ln_b[l]), a_w_s, jnp.transpose(a_b_s[l]),
                a_w_out, vec(mix_ln_g[l]), vec(mix_ln_b[l]), tm=1024)
        else:
            x3 = x2.reshape(bsz, seq, d)
            if l == N_A:
                kt, v2 = _kv_proj(x3, jnp.transpose(sb_w_k).astype(BF16),
                                  sb_w_v.astype(BF16), tm=1024)
            x2 = _sb_layer(x3, l - N_A, b_w_q, kt, v2, b_w_o,
                           vec(mix_ln_g[l]), vec(mix_ln_b[l]), tq=512).reshape(n, d)
        x2 = _mlp_layer(x2, l, ffn_w1, ffn_w2, vec(ffn_ln_g[l]), vec(ffn_ln_b[l]),
                        tm=1024, ff_chunk=1024)
    return x2.reshape(bsz, seq, d)
```
